```python
import math
import jax, jax.numpy as jnp
from jax import lax
import numpy as np

D_MODEL = 1024
BATCH = 2
SEQ = 8192
DEPTH = 2
DEC_BATCH = 32
DEC_SEQ = 1
PAST_LEN = 16384
PAGE_SIZE = 128

N_MIXERS = 2
N_ATTN_LAYERS = (DEPTH + 1) // 2
N_SSM_LAYERS = DEPTH // 2
N_HEADS = 8
HEAD_DIM = D_MODEL // N_HEADS // 2
ROT_DIM = HEAD_DIM // 4
ROPE_THETA = 500000.0
Q_BLOCK = 128
SSM_GROUP = 16
N_SSM_GROUPS = D_MODEL // SSM_GROUP
SSM_STATE = 64
N_EXPERTS = 16
N_EXPERT_GROUPS = 4
EXPERTS_PER_GROUP = N_EXPERTS // N_EXPERT_GROUPS
TOP_K = 2
D_EXPERT = 256
ALPHA = (2 * DEPTH) ** 0.25
BETA = (8 * DEPTH) ** -0.25
LN_EPS = 1e-5

kernel_name = "hybrid_diffattn_s5_moe_step"

F32 = jnp.float32


def layer_norm(x, g, b):
    xf = x.astype(F32)
    mu = jnp.mean(xf, axis=-1, keepdims=True)
    var = jnp.mean(jnp.square(xf - mu), axis=-1, keepdims=True)
    return ((xf - mu) * lax.rsqrt(var + LN_EPS) * g.astype(F32) + b.astype(F32)).astype(x.dtype)


def apply_rope(x, pos):
    half = ROT_DIM // 2
    inv = ROPE_THETA ** (-jnp.arange(half, dtype=F32) * 2.0 / ROT_DIM)
    ang = pos.astype(F32)[..., None] * inv
    cos = jnp.cos(ang)[:, :, None, None, :]
    sin = jnp.sin(ang)[:, :, None, None, :]
    xf = x.astype(F32)
    x1, x2, rest = xf[..., :half], xf[..., half:ROT_DIM], xf[..., ROT_DIM:]
    return jnp.concatenate([x1 * cos - x2 * sin, x2 * cos + x1 * sin, rest], axis=-1).astype(x.dtype)


def attn_qkv(x, w_qkv, pos):
    B, L, _ = x.shape
    q, k, v = jnp.split(x @ w_qkv, 3, axis=-1)
    q = apply_rope(q.reshape(B, L, N_HEADS, 2, HEAD_DIM), pos)
    k = apply_rope(k.reshape(B, L, N_HEADS, 2, HEAD_DIM), pos)
    v = v.reshape(B, L, N_HEADS, 2 * HEAD_DIM)
    return q, k, v


def diff_lambda(w_lambda, lam_init):
    wl = w_lambda.astype(F32)
    return jnp.exp(jnp.sum(wl[0] * wl[1])) - jnp.exp(jnp.sum(wl[2] * wl[3])) + lam_init


def diff_attn_prompt(q, k, v, lam):
    B, L = q.shape[:2]
    nb = L // Q_BLOCK
    scale = HEAD_DIM ** -0.5
    qb = q.reshape(B, nb, Q_BLOCK, N_HEADS, 2, HEAD_DIM).transpose(1, 0, 2, 3, 4, 5)
    kpos = jnp.arange(L)

    def block(args):
        qi, bi = args
        qpos = bi * Q_BLOCK + jnp.arange(Q_BLOCK)
        mask = kpos[None, :] <= qpos[:, None]
        s = jnp.einsum('bqhcd,bkhcd->bchqk', qi, k).astype(F32) * scale
        p = jax.nn.softmax(jnp.where(mask, s, -jnp.inf), axis=-1)
        a = p[:, 0] - lam * p[:, 1]
        return jnp.einsum('bhqk,bkhe->bqhe', a.astype(v.dtype), v)

    out = lax.map(block, (qb, jnp.arange(nb)))
    return out.transpose(1, 0, 2, 3, 4).reshape(B, L, N_HEADS, 2 * HEAD_DIM)


def diff_attn_sample(q, k_new, v_new, cache_k, cache_v, layer_idx, page_table, lam):
    scale = HEAD_DIM ** -0.5

    def one(args):
        qi, kn, vn, pt = args
        kp = cache_k[layer_idx, pt].reshape(-1, N_HEADS, 2, HEAD_DIM)
        vp = cache_v[layer_idx, pt].reshape(-1, N_HEADS, 2 * HEAD_DIM)
        n_past = kp.shape[0]
        t = qi.shape[0]
        kk = jnp.concatenate([kp, kn.astype(kp.dtype)], axis=0)
        vv = jnp.concatenate([vp, vn.astype(vp.dtype)], axis=0)
        kidx = jnp.arange(n_past + t)
        mask = (kidx[None, :] < n_past) | (kidx[None, :] - n_past <= jnp.arange(t)[:, None])
        s = jnp.einsum('qhcd,khcd->chqk', qi, kk).astype(F32) * scale
        p = jax.nn.softmax(jnp.where(mask, s, -jnp.inf), axis=-1)
        a = p[0] - lam * p[1]
        return jnp.einsum('hqk,khe->qhe', a.astype(vv.dtype), vv)

    return lax.map(one, (q, k_new, v_new, page_table))


def attn_output(o, subln_g, w_o, lam_init):
    B, L = o.shape[:2]
    of = o.astype(F32)
    of = of * lax.rsqrt(jnp.mean(jnp.square(of), axis=-1, keepdims=True) + LN_EPS)
    of = of * subln_g.astype(F32) * (1.0 - lam_init)
    return of.astype(o.dtype).reshape(B, L, D_MODEL) @ w_o


def ssm_mix(u, h0_re, h0_im, a_re, a_im, log_step, b_re, b_im, c_re, c_im, d_skip, w_glu):
    B, L, _ = u.shape
    ar, ai = a_re.astype(F32), a_im.astype(F32)
    dt = jnp.exp(log_step.astype(F32))[:, None]
    mag = jnp.exp(ar * dt)
    abr, abi = mag * jnp.cos(ai * dt), mag * jnp.sin(ai * dt)
    den = ar * ar + ai * ai
    cr = ((abr - 1.0) * ar + abi * ai) / den
    ci = (abi * ar - (abr - 1.0) * ai) / den
    ug = u.reshape(B, L, N_SSM_GROUPS, SSM_GROUP).astype(F32)
    bu_r = jnp.einsum('gpc,blgc->blgp', b_re.astype(F32), ug)
    bu_i = jnp.einsum('gpc,blgc->blgp', b_im.astype(F32), ug)
    br = cr * bu_r - ci * bu_i
    bi = cr * bu_i + ci * bu_r
    h0r, h0i = h0_re.astype(F32), h0_im.astype(F32)
    br = br.at[:, 0].add(abr * h0r - abi * h0i)
    bi = bi.at[:, 0].add(abr * h0i + abi * h0r)
    Ar = jnp.broadcast_to(abr, br.shape)
    Ai = jnp.broadcast_to(abi, bi.shape)

    def combine(e1, e2):
        a1r, a1i, b1r, b1i = e1
        a2r, a2i, b2r, b2i = e2
        return (a1r * a2r - a1i * a2i, a1r * a2i + a1i * a2r,
                a2r * b1r - a2i * b1i + b2r, a2r * b1i + a2i * b1r + b2i)

    _, _, hr, hi = lax.associative_scan(combine, (Ar, Ai, br, bi), axis=1)
    y = (jnp.einsum('gcp,blgp->blgc', c_re.astype(F32), hr)
         - jnp.einsum('gcp,blgp->blgc', c_im.astype(F32), hi)).reshape(B, L, D_MODEL)
    y = y + d_skip.astype(F32) * u.astype(F32)
    g = jax.nn.gelu(y).astype(u.dtype)
    val, gate = jnp.split(g @ w_glu, 2, axis=-1)
    out = val * jax.nn.sigmoid(gate)
    return out, hr[:, -1], hi[:, -1]


def moe(x, w_router, router_bias, w_in, w_out):
    B, L, _ = x.shape
    t = x.reshape(-1, D_MODEL)
    scores = jax.nn.sigmoid((t @ w_router).astype(F32))
    sel = scores + router_bias.astype(F32)
    grp = sel.reshape(-1, N_EXPERT_GROUPS, EXPERTS_PER_GROUP)
    grp_score = jnp.sum(lax.top_k(grp, TOP_K)[0], axis=-1)
    g_idx = jnp.argmax(grp_score, axis=-1)
    in_grp = jnp.take_along_axis(grp, g_idx[:, None, None], axis=1)[:, 0]
    _, local = lax.top_k(in_grp, TOP_K)
    experts = g_idx[:, None] * EXPERTS_PER_GROUP + local
    w = jnp.take_along_axis(scores, experts, axis=1)
    w = w / jnp.sum(w, axis=-1, keepdims=True)
    gates = jnp.sum(jax.nn.one_hot(experts, N_EXPERTS, dtype=F32) * w[..., None], axis=1)
    h = jnp.einsum('td,edf->tef', t, w_in)
    hg, hu = jnp.split(h, 2, axis=-1)
    h = jax.nn.silu(hg) * hu * gates[:, :, None].astype(h.dtype)
    return jnp.einsum('tef,efd->td', h, w_out).reshape(B, L, D_MODEL)


def setup_inputs(seed: int = 0) -> dict:
    key = jax.random.key(seed)
    ks = jax.random.split(key, 32)
    n_pages = PAST_LEN // PAGE_SIZE
    n_used = DEC_BATCH * n_pages
    n_pool = n_used + n_used // 4
    nrm = jax.random.normal
    dsc = D_MODEL ** -0.5
    qkv_scale = jnp.concatenate([jnp.full((2 * D_MODEL,), dsc), jnp.full((D_MODEL,), dsc * BETA)])
    log_lo, log_hi = math.log(1e-3), math.log(1e-1)
    return {
        "x_prompt": nrm(ks[0], (BATCH, SEQ, D_MODEL), F32),
        "x_sample": nrm(ks[1], (DEC_BATCH, DEC_SEQ, D_MODEL), F32),
        "cache_k": nrm(ks[2], (N_ATTN_LAYERS, n_pool, PAGE_SIZE, N_HEADS, 2, HEAD_DIM), F32),
        "cache_v": nrm(ks[3], (N_ATTN_LAYERS, n_pool, PAGE_SIZE, N_HEADS, 2 * HEAD_DIM), F32),
        "state_ssm_re": 0.5 * nrm(ks[4], (N_SSM_LAYERS, DEC_BATCH, N_SSM_GROUPS, SSM_STATE), F32),
        "state_ssm_im": 0.5 * nrm(ks[5], (N_SSM_LAYERS, DEC_BATCH, N_SSM_GROUPS, SSM_STATE), F32),
        "page_table": jax.random.permutation(ks[6], n_pool)[:n_used].reshape(DEC_BATCH, n_pages).astype(jnp.int32),
        "w_qkv": nrm(ks[7], (N_ATTN_LAYERS, D_MODEL, 3 * D_MODEL), F32) * qkv_scale,
        "w_lambda": 0.1 * nrm(ks[8], (N_ATTN_LAYERS, 4, HEAD_DIM), F32),
        "attn_subln": 1.0 + 0.02 * nrm(ks[9], (N_ATTN_LAYERS, 2 * HEAD_DIM), F32),
        "w_attn_out": nrm(ks[10], (N_ATTN_LAYERS, D_MODEL, D_MODEL), F32) * (dsc * BETA),
        "ssm_a_re": -0.5 + 0.01 * nrm(ks[11], (N_SSM_LAYERS, N_SSM_GROUPS, SSM_STATE), F32),
        "ssm_a_im": jnp.pi * jnp.arange(SSM_STATE, dtype=F32) + 0.01 * nrm(ks[12], (N_SSM_LAYERS, N_SSM_GROUPS, SSM_STATE), F32),
        "ssm_log_step": jax.random.uniform(ks[13], (N_SSM_LAYERS, N_SSM_GROUPS), F32, log_lo, log_hi),
        "ssm_b_re": nrm(ks[14], (N_SSM_LAYERS, N_SSM_GROUPS, SSM_STATE, SSM_GROUP), F32) * SSM_GROUP ** -0.5,
        "ssm_b_im": nrm(ks[15], (N_SSM_LAYERS, N_SSM_GROUPS, SSM_STATE, SSM_GROUP), F32) * SSM_GROUP ** -0.5,
        "ssm_c_re": nrm(ks[16], (N_SSM_LAYERS, N_SSM_GROUPS, SSM_GROUP, SSM_STATE), F32) * SSM_STATE ** -0.5,
        "ssm_c_im": nrm(ks[17], (N_SSM_LAYERS, N_SSM_GROUPS, SSM_GROUP, SSM_STATE), F32) * SSM_STATE ** -0.5,
        "ssm_d": nrm(ks[18], (N_SSM_LAYERS, D_MODEL), F32),
        "w_glu": nrm(ks[19], (N_SSM_LAYERS, D_MODEL, 2 * D_MODEL), F32) * (dsc * BETA),
        "w_router": nrm(ks[20], (D_MODEL, N_EXPERTS), F32) * dsc,
        "router_bias": 0.01 * nrm(ks[21], (N_EXPERTS,), F32),
        "w_expert_in": nrm(ks[22], (DEPTH, N_EXPERTS, D_MODEL, 2 * D_EXPERT), F32) * (dsc * BETA),
        "w_expert_out": nrm(ks[23], (DEPTH, N_EXPERTS, D_EXPERT, D_MODEL), F32) * (D_EXPERT ** -0.5 * BETA),
        "ln_gain": 1.0 + 0.02 * nrm(ks[24], (DEPTH, 2, D_MODEL), F32),
        "ln_bias": 0.02 * nrm(ks[25], (DEPTH, 2, D_MODEL), F32),
    }


def reference(x_prompt, x_sample, cache_k, cache_v, state_ssm_re, state_ssm_im, page_table,
              w_qkv, w_lambda, attn_subln, w_attn_out,
              ssm_a_re, ssm_a_im, ssm_log_step, ssm_b_re, ssm_b_im, ssm_c_re, ssm_c_im, ssm_d, w_glu,
              w_router, router_bias, w_expert_in, w_expert_out, ln_gain, ln_bias):
    bp, lp = x_prompt.shape[:2]
    bs, ls = x_sample.shape[:2]
    past_len = page_table.shape[1] * PAGE_SIZE
    pos_p = jnp.broadcast_to(jnp.arange(lp, dtype=jnp.int32), (bp, lp))
    pos_s = jnp.broadcast_to(past_len + jnp.arange(ls, dtype=jnp.int32), (bs, ls))
    hp, hs = x_prompt, x_sample
    k_p, v_p, sr_p, si_p = [], [], [], []
    k_s, v_s, sr_s, si_s = [], [], [], []
    for layer in range(DEPTH):
        j = layer // N_MIXERS
        if layer % N_MIXERS == 0:
            lam_init = 0.8 - 0.6 * math.exp(-0.3 * layer)
            lam = diff_lambda(w_lambda[j], lam_init)
            qp, kp, vp = attn_qkv(hp, w_qkv[j], pos_p)
            qs, kq, vq = attn_qkv(hs, w_qkv[j], pos_s)
            op = diff_attn_prompt(qp, kp, vp, lam)
            osm = diff_attn_sample(qs, kq, vq, cache_k, cache_v, j, page_table, lam)
            mix_p = attn_output(op, attn_subln[j], w_attn_out[j], lam_init)
            mix_s = attn_output(osm, attn_subln[j], w_attn_out[j], lam_init)
            k_p.append(kp); v_p.append(vp); k_s.append(kq); v_s.append(vq)
        else:
            prm = (ssm_a_re[j], ssm_a_im[j], ssm_log_step[j], ssm_b_re[j], ssm_b_im[j],
                   ssm_c_re[j], ssm_c_im[j], ssm_d[j], w_glu[j])
            zeros = jnp.zeros((bp, N_SSM_GROUPS, SSM_STATE), F32)
            mix_p, hr_p, hi_p = ssm_mix(hp, zeros, zeros, *prm)
            mix_s, hr_s, hi_s = ssm_mix(hs, state_ssm_re[j], state_ssm_im[j], *prm)
            sr_p.append(hr_p); si_p.append(hi_p); sr_s.append(hr_s); si_s.append(hi_s)
        hp = layer_norm(ALPHA * hp + mix_p, ln_gain[layer, 0], ln_bias[layer, 0])
        hs = layer_norm(ALPHA * hs + mix_s, ln_gain[layer, 0], ln_bias[layer, 0])
        hp = layer_norm(ALPHA * hp + moe(hp, w_router, router_bias, w_expert_in[layer], w_expert_out[layer]),
                        ln_gain[layer, 1], ln_bias[layer, 1])
        hs = layer_norm(ALPHA * hs + moe(hs, w_router, router_bias, w_expert_in[layer], w_expert_out[layer]),
                        ln_gain[layer, 1], ln_bias[layer, 1])
    return (hp, hs,
            jnp.stack(k_p), jnp.stack(v_p), jnp.stack(sr_p), jnp.stack(si_p),
            jnp.stack(k_s), jnp.stack(v_s), jnp.stack(sr_s), jnp.stack(si_s))
```

```python
import functools
import math

import jax
import jax.numpy as jnp
from jax import lax
from jax.experimental import pallas as pl
from jax.experimental.pallas import tpu as pltpu

F32 = jnp.float32
BF16 = jnp.bfloat16

D_MODEL = 1024
DEPTH = 2
PAGE_SIZE = 128
N_HEADS = 8
HEAD_DIM = 64
V_DIM = 2 * HEAD_DIM
ROT_DIM = HEAD_DIM // 4
ROPE_THETA = 500000.0
SSM_GROUP = 16
N_SSM_GROUPS = D_MODEL // SSM_GROUP
SSM_STATE = 64
N_STATES = N_SSM_GROUPS * SSM_STATE
N_EXPERTS = 16
N_EXPERT_GROUPS = 4
EXPERTS_PER_GROUP = N_EXPERTS // N_EXPERT_GROUPS
D_EXPERT = 256
ALPHA = (2 * DEPTH) ** 0.25
LN_EPS = 1e-5
QK_SCALE = HEAD_DIM ** -0.5

LANES = 128
SUBLANES = 8
MXU_DIM = 256
VMEM_LIMIT = 56 * 1024 * 1024

QKV_TM = 256
ATT_TQ = 256
ROW_TM = 512
MOE_TM = 256
SSM_TC = 256
SSM_NA = SSM_TC // SUBLANES
SSM_LW = 512
SSM_NLG = N_STATES // SSM_LW
SSM_BLK = 4
DEC_NP = 4


def _cparams(*sem):
    return pltpu.CompilerParams(dimension_semantics=sem, vmem_limit_bytes=VMEM_LIMIT)


def _layer_norm(z, g, b):
    mu = jnp.mean(z, axis=-1, keepdims=True)
    d = z - mu
    var = jnp.mean(d * d, axis=-1, keepdims=True)
    return d * lax.rsqrt(var + LN_EPS) * g + b


def _dot(a, b):
    return jnp.dot(a, b, preferred_element_type=F32)


def _dot_nt(a, b):
    return lax.dot_general(a, b, (((1,), (1,)), ((), ())), preferred_element_type=F32)


def _rope(x, cos, s_up, s_dn):
    outs = []
    for h in range(N_HEADS):
        xh = x[:, h * V_DIM:(h + 1) * V_DIM]
        up = pltpu.roll(xh, V_DIM - ROT_DIM // 2, 1)
        dn = pltpu.roll(xh, ROT_DIM // 2, 1)
        outs.append(xh * cos + up * s_up + dn * s_dn)
    return outs


def _qkv_kernel(x_ref, w_ref, cos_ref, sup_ref, sdn_ref, *out_refs, head_major):
    y = _dot(x_ref[...].astype(BF16), w_ref[...])
    cos, s_up, s_dn = cos_ref[...], sup_ref[...], sdn_ref[...]
    q = _rope(y[:, :D_MODEL], cos, s_up, s_dn)
    k = _rope(y[:, D_MODEL:2 * D_MODEL], cos, s_up, s_dn)
    v = y[:, 2 * D_MODEL:]
    if head_major:
        k_ref, v_ref, qm_ref, khm_ref, vhm_ref = out_refs
        lane = lax.broadcasted_iota(jnp.int32, (1, V_DIM), 1)
        first = lane < HEAD_DIM
        for h in range(N_HEADS):
            qh = q[h] * QK_SCALE
            qm_ref[0, h] = jnp.where(first, qh, 0.0).astype(BF16)
            qm_ref[1, h] = jnp.where(first, 0.0, qh).astype(BF16)
            khm_ref[h] = k[h].astype(BF16)
            vhm_ref[h] = v[:, h * V_DIM:(h + 1) * V_DIM].astype(BF16)
    else:
        k_ref, v_ref, q_ref = out_refs
        for h in range(N_HEADS):
            q_ref[:, h * V_DIM:(h + 1) * V_DIM] = q[h] * QK_SCALE
    for h in range(N_HEADS):
        k_ref[:, h * V_DIM:(h + 1) * V_DIM] = k[h]
    v_ref[...] = v


def _qkv_call(x, w_bf16, tables, *, tm, table_blocks, head_major):
    t = x.shape[0]
    row = pl.BlockSpec((tm, D_MODEL), lambda i: (i, 0))
    tab = pl.BlockSpec((tm, V_DIM), lambda i: (i % table_blocks, 0))
    out_shape = [jax.ShapeDtypeStruct((t, D_MODEL), F32), jax.ShapeDtypeStruct((t, D_MODEL), F32)]
    out_specs = [row, row]
    if head_major:
        hm = pl.BlockSpec((N_HEADS, tm, V_DIM), lambda i: (0, i, 0))
        out_shape += [jax.ShapeDtypeStruct((2, N_HEADS, t, V_DIM), BF16),
                      jax.ShapeDtypeStruct((N_HEADS, t, V_DIM), BF16),
                      jax.ShapeDtypeStruct((N_HEADS, t, V_DIM), BF16)]
        out_specs += [pl.BlockSpec((2, N_HEADS, tm, V_DIM), lambda i: (0, 0, i, 0)), hm, hm]
    else:
        out_shape += [jax.ShapeDtypeStruct((t, D_MODEL), F32)]
        out_specs += [row]
    return pl.pallas_call(
        functools.partial(_qkv_kernel, head_major=head_major),
        grid=(t // tm,),
        in_specs=[row, pl.BlockSpec((D_MODEL, 3 * D_MODEL), lambda i: (0, 0)), tab, tab, tab],
        out_specs=out_specs,
        out_shape=out_shape,
        compiler_params=_cparams("parallel"),
        name="qkv_rope",
    )(x, w_bf16, *tables)


def _rope_tables(pos):
    half = ROT_DIM // 2
    inv = ROPE_THETA ** (-jnp.arange(half, dtype=F32) * 2.0 / ROT_DIM)
    ang = pos.astype(F32)[:, None] * inv
    cos, sin = jnp.cos(ang), jnp.sin(ang)
    n = pos.shape[0]
    ones = jnp.ones((n, HEAD_DIM - ROT_DIM), F32)
    zeros = jnp.zeros((n, HEAD_DIM - ROT_DIM), F32)
    z8 = jnp.zeros((n, half), F32)
    c = jnp.concatenate([cos, cos, ones], axis=1)
    s_up = jnp.concatenate([-sin, z8, zeros], axis=1)
    s_dn = jnp.concatenate([z8, sin, zeros], axis=1)
    return tuple(jnp.tile(a, (1, 2)) for a in (c, s_up, s_dn))


def _diff_lambda(wl, lam_init):
    a = jnp.sum(wl[0:1] * wl[1:2], axis=-1, keepdims=True)
    b = jnp.sum(wl[2:3] * wl[3:4], axis=-1, keepdims=True)
    return jnp.exp(a) - jnp.exp(b) + lam_init


def _sub_norm(o, g, lam_init):
    o = o * lax.rsqrt(jnp.mean(o * o, axis=-1, keepdims=True) + LN_EPS)
    return o * g * (1.0 - lam_init)


def _softmax_step(s, v, m_ref, l_ref, acc_ref):
    m_prev = m_ref[...]
    m_new = jnp.maximum(m_prev, jnp.max(s, axis=-1, keepdims=True))
    alpha = jnp.exp(m_prev - m_new)
    p = jnp.exp(s - m_new[:, :1])
    l_ref[...] = alpha * l_ref[...] + jnp.sum(p, axis=-1, keepdims=True)
    m_ref[...] = m_new
    return alpha, p


def _attn_kernel(wl_ref, g_ref, q_ref, k_ref, v_ref, o_ref, m_ref, l_ref, acc_ref, *, tq, lam_init):
    qi = pl.program_id(2)
    q = q_ref[...].reshape(2 * tq, V_DIM)
    m_ref[...] = jnp.full(m_ref.shape, -jnp.inf, F32)
    l_ref[...] = jnp.zeros(l_ref.shape, F32)
    acc_ref[...] = jnp.zeros(acc_ref.shape, F32)

    def step(j, masked):
        start = pl.multiple_of(j * tq, tq)
        k = k_ref[0, pl.ds(start, tq), :]
        v = v_ref[0, pl.ds(start, tq), :]
        s = _dot_nt(q, k)
        if masked:
            row = lax.broadcasted_iota(jnp.int32, s.shape, 0)
            col = lax.broadcasted_iota(jnp.int32, s.shape, 1)
            row = jnp.where(row >= tq, row - tq, row)
            s = jnp.where(col <= row, s, -jnp.inf)
        alpha, p = _softmax_step(s, v, m_ref, l_ref, acc_ref)
        acc_ref[...] = alpha * acc_ref[...] + _dot(p.astype(BF16), v)

    def body(j, carry):
        step(j, False)
        return carry

    lax.fori_loop(0, qi, body, 0)
    step(qi, True)

    out = acc_ref[...] / l_ref[...]
    lam = _diff_lambda(wl_ref[...], lam_init)
    o = out[:tq] - lam * out[tq:]
    o_ref[...] = _sub_norm(o, g_ref[...], lam_init).astype(o_ref.dtype)


def _attn_call(qm, khm, vhm, w_lambda, subln, *, batch, seq, lam_init):
    tq = ATT_TQ
    nq = seq // tq
    kv = pl.BlockSpec((1, seq, V_DIM), lambda b, h, i: (h, b, 0))
    return pl.pallas_call(
        functools.partial(_attn_kernel, tq=tq, lam_init=lam_init),
        grid=(batch, N_HEADS, nq),
        in_specs=[pl.BlockSpec((4, HEAD_DIM), lambda b, h, i: (0, 0)),
                  pl.BlockSpec((1, V_DIM), lambda b, h, i: (0, 0)),
                  pl.BlockSpec((2, 1, tq, V_DIM), lambda b, h, i: (0, h, b * nq + i, 0)),
                  kv, kv],
        out_specs=pl.BlockSpec((tq, V_DIM), lambda b, h, i: (b * nq + i, h)),
        out_shape=jax.ShapeDtypeStruct((batch * seq, D_MODEL), BF16),
        scratch_shapes=[pltpu.VMEM((2 * tq, V_DIM), F32)] * 3,
        compiler_params=_cparams("parallel", "parallel", "arbitrary"),
        name="attn_prompt",
    )(w_lambda, subln, qm, khm, vhm)


def _dec_attn_kernel(pt_ref, wl_ref, g_ref, q_ref, kn_ref, vn_ref, e_ref, *rest, n_pages, lam_init):
    k_refs = rest[:DEC_NP]
    v_refs = rest[DEC_NP:2 * DEC_NP]
    o_ref, qrow_ref, m_ref, l_ref, acc_ref = rest[2 * DEC_NP:]
    step_id = pl.program_id(1)
    n_rows = 2 * N_HEADS

    @pl.when(step_id == 0)
    def _():
        m_ref[...] = jnp.full(m_ref.shape, -jnp.inf, F32)
        l_ref[...] = jnp.zeros(l_ref.shape, F32)
        acc_ref[...] = jnp.zeros(acc_ref.shape, F32)
        row = lax.broadcasted_iota(jnp.int32, (n_rows, D_MODEL), 0)
        lane = lax.broadcasted_iota(jnp.int32, (n_rows, D_MODEL), 1)
        block = (row & (N_HEADS - 1)) * 2 + (row >> 3)
        qrow_ref[...] = jnp.where((lane >> 6) == block, q_ref[0], 0.0)

    qrows = qrow_ref[...]
    qb = qrows.astype(BF16)
    row = lax.broadcasted_iota(jnp.int32, (n_rows, PAGE_SIZE * N_HEADS), 0)
    lane = lax.broadcasted_iota(jnp.int32, (n_rows, PAGE_SIZE * N_HEADS), 1)
    own_head = (lane & (N_HEADS - 1)) == (row & (N_HEADS - 1))
    for i in range(DEC_NP):
        s = _dot(qb, k_refs[i][0, 0].astype(BF16))
        alpha, p = _softmax_step(s, None, m_ref, l_ref, acc_ref)
        pexp = _dot(p.astype(BF16), e_ref[...])
        pexp = jnp.where(own_head, pexp, 0.0).astype(BF16)
        acc_ref[...] = alpha * acc_ref[...] + _dot(pexp, v_refs[i][0, 0].astype(BF16))

    @pl.when(step_id == n_pages // DEC_NP - 1)
    def _():
        s = jnp.sum(qrows * kn_ref[0], axis=-1, keepdims=True)
        m_prev = m_ref[...]
        m_new = jnp.maximum(m_prev, s)
        alpha = jnp.exp(m_prev - m_new)
        p = jnp.exp(s - m_new[:, :1])
        l = alpha * l_ref[...] + p
        vn = vn_ref[0]
        acc = alpha * acc_ref[...] + p * jnp.concatenate([vn, vn], axis=0)
        out = acc / l
        lam = _diff_lambda(wl_ref[...], lam_init)
        o = out[:N_HEADS] - lam * out[N_HEADS:]
        o_ref[0] = _sub_norm(o, g_ref[...], lam_init).astype(o_ref.dtype)


def _dec_attn_call(page_table, q, k_new, v_new, cache_k, cache_v, w_lambda, subln, *, lam_init):
    nseq, n_pages = page_table.shape
    row = pl.BlockSpec((1, 1, D_MODEL), lambda b, p, pt: (b, 0, 0))
    heads = pl.BlockSpec((1, N_HEADS, V_DIM), lambda b, p, pt: (b, 0, 0))

    def page_spec(i):
        return pl.BlockSpec((1, 1, D_MODEL, PAGE_SIZE), lambda b, p, pt: (0, pt[b, p * DEC_NP + i], 0, 0))

    pages = [page_spec(i) for i in range(DEC_NP)]
    expand = (jnp.arange(PAGE_SIZE * N_HEADS)[None, :] // N_HEADS == jnp.arange(PAGE_SIZE)[:, None]).astype(BF16)
    grid_spec = pltpu.PrefetchScalarGridSpec(
        num_scalar_prefetch=1,
        grid=(nseq, n_pages // DEC_NP),
        in_specs=[pl.BlockSpec((4, HEAD_DIM), lambda b, p, pt: (0, 0)),
                  pl.BlockSpec((1, V_DIM), lambda b, p, pt: (0, 0)),
                  row, row, heads,
                  pl.BlockSpec((PAGE_SIZE, PAGE_SIZE * N_HEADS), lambda b, p, pt: (0, 0))] + pages + pages,
        out_specs=heads,
        scratch_shapes=[pltpu.VMEM((2 * N_HEADS, D_MODEL), F32),
                        pltpu.VMEM((2 * N_HEADS, LANES), F32),
                        pltpu.VMEM((2 * N_HEADS, LANES), F32),
                        pltpu.VMEM((2 * N_HEADS, V_DIM), F32)],
    )
    r3 = lambda a: a.reshape(nseq, 1, D_MODEL)
    out = pl.pallas_call(
        functools.partial(_dec_attn_kernel, n_pages=n_pages, lam_init=lam_init),
        grid_spec=grid_spec,
        out_shape=jax.ShapeDtypeStruct((nseq, N_HEADS, V_DIM), BF16),
        compiler_params=_cparams("parallel", "arbitrary"),
        name="attn_decode",
    )(page_table, w_lambda, subln, r3(q), r3(k_new), v_new.reshape(nseq, N_HEADS, V_DIM), expand,
      *([cache_k] * DEC_NP), *([cache_v] * DEC_NP))
    return out.reshape(nseq, D_MODEL)


def _proj_ln_kernel(o_ref, w_ref, x_ref, g_ref, b_ref, out_ref):
    z = ALPHA * x_ref[...] + _dot(o_ref[...], w_ref[...])
    out_ref[...] = _layer_norm(z, g_ref[...], b_ref[...])


def _proj_ln_call(o, w_bf16, x, g, b, *, tm):
    t = x.shape[0]
    row = pl.BlockSpec((tm, D_MODEL), lambda i: (i, 0))
    vec = pl.BlockSpec((1, D_MODEL), lambda i: (0, 0))
    return pl.pallas_call(
        _proj_ln_kernel,
        grid=(t // tm,),
        in_specs=[row, pl.BlockSpec((D_MODEL, D_MODEL), lambda i: (0, 0)), row, vec, vec],
        out_specs=row,
        out_shape=jax.ShapeDtypeStruct((t, D_MODEL), F32),
        compiler_params=_cparams("parallel"),
        name="proj_ln",
    )(o, w_bf16, x, g, b)


def _route(scores, sel):
    def pair_max(vals):
        best = None
        for i in range(len(vals)):
            for j in range(i + 1, len(vals)):
                s = vals[i] + vals[j]
                best = s if best is None else jnp.maximum(best, s)
        return best

    grp = [sel[g * EXPERTS_PER_GROUP:(g + 1) * EXPERTS_PER_GROUP] for g in range(N_EXPERT_GROUPS)]
    grp_score = [pair_max(v) for v in grp]
    best, g_idx = grp_score[0], jnp.zeros_like(grp_score[0], dtype=jnp.int32)
    for g in range(1, N_EXPERT_GROUPS):
        upd = grp_score[g] > best
        g_idx = jnp.where(upd, g, g_idx)
        best = jnp.where(upd, grp_score[g], best)

    def pick(rows, j):
        out = rows[j]
        for g in range(1, N_EXPERT_GROUPS):
            out = jnp.where(g_idx == g, rows[g * EXPERTS_PER_GROUP + j], out)
        return out

    in_sel = [pick(sel, j) for j in range(EXPERTS_PER_GROUP)]
    in_sc = [pick(scores, j) for j in range(EXPERTS_PER_GROUP)]

    def arg_first_max(vals, excluded):
        bv, bi = None, None
        for j, v in enumerate(vals):
            v = v if excluded is None else jnp.where(excluded == j, -jnp.inf, v)
            if bv is None:
                bv, bi = v, jnp.zeros_like(v, dtype=jnp.int32)
            else:
                upd = v > bv
                bi = jnp.where(upd, j, bi)
                bv = jnp.where(upd, v, bv)
        return bi

    l1 = arg_first_max(in_sel, None)
    l2 = arg_first_max(in_sel, l1)

    def take(vals, idx):
        out = vals[0]
        for j in range(1, len(vals)):
            out = jnp.where(idx == j, vals[j], out)
        return out

    w1, w2 = take(in_sc, l1), take(in_sc, l2)
    tot = w1 + w2
    w1, w2 = w1 / tot, w2 / tot
    gates = []
    for e in range(N_EXPERTS):
        g, j = divmod(e, EXPERTS_PER_GROUP)
        local = jnp.where(l1 == j, w1, jnp.where(l2 == j, w2, 0.0))
        gates.append(jnp.where(g_idx == g, local, 0.0))
    return gates


def _moe_ln_kernel(x_ref, wr_ref, rb_ref, win_ref, wout_ref, g_ref, b_ref, out_ref, hcat_ref):
    x = x_ref[...]
    tm = x.shape[0]
    logits = jnp.dot(x, wr_ref[...], preferred_element_type=F32, precision=lax.Precision.HIGHEST)
    scores_t = jax.nn.sigmoid(logits.T[:N_EXPERTS])
    sel_t = scores_t + rb_ref[...]
    rows = lambda a: [a[e:e + 1] for e in range(N_EXPERTS)]
    gates = _route(rows(scores_t), rows(sel_t))
    gates_t = jnp.concatenate(gates + [jnp.zeros((LANES - N_EXPERTS, tm), F32)], axis=0)
    gate_cols = gates_t.T
    xb = x.astype(BF16)
    for e in range(N_EXPERTS):
        h = _dot(xb, win_ref[e])
        a = jax.nn.silu(h[:, :D_EXPERT]) * h[:, D_EXPERT:] * gate_cols[:, e:e + 1]
        hcat_ref[:, e * D_EXPERT:(e + 1) * D_EXPERT] = a.astype(BF16)
    z = ALPHA * x + _dot(hcat_ref[...], wout_ref[...])
    out_ref[...] = _layer_norm(z, g_ref[...], b_ref[...])


def _moe_ln_call(x, wr_pad, rb_col, win_bf16, wout_bf16, g, b, *, tm):
    t = x.shape[0]
    row = pl.BlockSpec((tm, D_MODEL), lambda i: (i, 0))
    vec = pl.BlockSpec((1, D_MODEL), lambda i: (0, 0))
    once = pl.Buffered(1)
    return pl.pallas_call(
        _moe_ln_kernel,
        grid=(t // tm,),
        in_specs=[row,
                  pl.BlockSpec((D_MODEL, LANES), lambda i: (0, 0)),
                  pl.BlockSpec((N_EXPERTS, 1), lambda i: (0, 0)),
                  pl.BlockSpec((N_EXPERTS, D_MODEL, 2 * D_EXPERT), lambda i: (0, 0, 0), pipeline_mode=once),
                  pl.BlockSpec((N_EXPERTS * D_EXPERT, D_MODEL), lambda i: (0, 0), pipeline_mode=once),
                  vec, vec],
        out_specs=row,
        out_shape=jax.ShapeDtypeStruct((t, D_MODEL), F32),
        scratch_shapes=[pltpu.VMEM((tm, N_EXPERTS * D_EXPERT), BF16)],
        compiler_params=_cparams("parallel"),
        name="moe_ln",
    )(x, wr_pad, rb_col, win_bf16, wout_bf16, g, b)


def _ssm_param_kernel(ar_ref, ai_ref, ls_ref, bre_ref, bim_ref,
                      abr_ref, abi_ref, anr_ref, ani_ref, wre_ref, wim_ref, *, n_sq):
    ar, ai = ar_ref[...], ai_ref[...]
    dt = jnp.exp(ls_ref[...])
    mag = jnp.exp(ar * dt)
    abr, abi = mag * jnp.cos(ai * dt), mag * jnp.sin(ai * dt)
    den = ar * ar + ai * ai
    cr = ((abr - 1.0) * ar + abi * ai) / den
    ci = (abi * ar - (abr - 1.0) * ai) / den
    bre, bim = bre_ref[...], bim_ref[...]
    wre_ref[...] = cr * bre - ci * bim
    wim_ref[...] = cr * bim + ci * bre
    abr_ref[...] = abr
    abi_ref[...] = abi
    pr, pi = abr, abi
    for _ in range(n_sq):
        pr, pi = pr * pr - pi * pi, 2.0 * pr * pi
    anr_ref[...] = pr
    ani_ref[...] = pi


def _ssm_param_call(a_re, a_im, log_step, b_re, b_im):
    col = lambda a: a.reshape(N_STATES, 1)
    ls = jnp.broadcast_to(log_step[:, None], (N_SSM_GROUPS, SSM_STATE))
    wide = lambda a: a.reshape(N_STATES, SSM_GROUP)
    cshape = jax.ShapeDtypeStruct((N_STATES, 1), F32)
    wshape = jax.ShapeDtypeStruct((N_STATES, SSM_GROUP), F32)
    return pl.pallas_call(
        functools.partial(_ssm_param_kernel, n_sq=int(math.log2(SSM_NA))),
        out_shape=[cshape] * 4 + [wshape] * 2,
        compiler_params=pltpu.CompilerParams(vmem_limit_bytes=VMEM_LIMIT),
        name="ssm_params",
    )(col(a_re), col(a_im), col(ls), wide(b_re), wide(b_im))


def _block_diag_in(w):
    gl = N_SSM_GROUPS // SSM_BLK
    w = w.reshape(SSM_BLK, gl, SSM_STATE, SSM_GROUP)
    eye = jnp.eye(gl, dtype=w.dtype)
    full = w.transpose(0, 1, 3, 2)[:, :, :, None, :] * eye[None, :, None, :, None]
    return full.reshape(SSM_BLK, gl * SSM_GROUP, gl * SSM_STATE)


def _block_diag_out(c):
    gl = N_SSM_GROUPS // SSM_BLK
    c = c.reshape(SSM_BLK, gl, SSM_GROUP, SSM_STATE)
    eye = jnp.eye(gl, dtype=c.dtype)
    full = c.transpose(0, 1, 3, 2)[:, :, :, None, :] * eye[None, :, None, :, None]
    return full.reshape(SSM_BLK, gl * SSM_STATE, gl * SSM_GROUP)


def _glu_ln(u, y, d_ref, wglu_ref, g_ref, b_ref):
    y = y + d_ref[...] * u
    gl = jax.nn.gelu(y).astype(BF16)
    vg = _dot(gl, wglu_ref[...])
    mix = vg[:, :D_MODEL] * jax.nn.sigmoid(vg[:, D_MODEL:])
    return _layer_norm(ALPHA * u + mix, g_ref[...], b_ref[...])


def _ssm_scan_kernel(u_ref, abr_ref, abi_ref, anr_ref, ani_ref, wb_ref, wc_ref, d_ref, wglu_ref, g_ref, b_ref,
                     out_ref, hre_ref, him_ref, u_scr, sre_scr, sim_scr, cre_scr, cim_scr):
    chunk = pl.program_id(1)
    na, tc = SSM_NA, SSM_TC

    @pl.when(chunk == 0)
    def _():
        cre_scr[...] = jnp.zeros(cre_scr.shape, F32)
        cim_scr[...] = jnp.zeros(cim_scr.shape, F32)

    ncol = D_MODEL // LANES
    for a in range(na):
        for c in range(ncol):
            u_scr[a * SUBLANES:(a + 1) * SUBLANES, c * LANES:(c + 1) * LANES] = (
                u_ref[pl.ds(a * ncol + c, SUBLANES, stride=ncol * na), :])
    u = u_scr[...]
    ub = u.astype(BF16)
    half = N_STATES // SSM_BLK
    for j in range(SSM_BLK):
        bu = _dot(ub[:, j * MXU_DIM:(j + 1) * MXU_DIM], wb_ref[j])
        for w in range(half // SSM_LW):
            lg = j * (half // SSM_LW) + w
            sre_scr[lg] = bu[:, w * SSM_LW:(w + 1) * SSM_LW]
            sim_scr[lg] = bu[:, half + w * SSM_LW:half + (w + 1) * SSM_LW]

    def lane_group(lg, carry):
        bshape = (SUBLANES, SSM_LW)
        ar = jnp.broadcast_to(abr_ref[lg], bshape)
        ai = jnp.broadcast_to(abi_ref[lg], bshape)

        def rows(a):
            return pl.ds(pl.multiple_of(a * SUBLANES, SUBLANES), SUBLANES)

        def pass1(a, h):
            hr, hi = h
            return (ar * hr - ai * hi + sre_scr[lg, rows(a), :], ar * hi + ai * hr + sim_scr[lg, rows(a), :])

        zero = jnp.zeros(bshape, F32)
        er, ei = lax.fori_loop(0, na, pass1, (zero, zero), unroll=4)

        anr, ani = anr_ref[lg], ani_ref[lg]
        sr, si = cre_scr[lg], cim_scr[lg]
        starts_r, starts_i = [], []
        for r in range(SUBLANES):
            starts_r.append(sr)
            starts_i.append(si)
            sr, si = (anr * sr - ani * si + er[r:r + 1], anr * si + ani * sr + ei[r:r + 1])
        cre_scr[lg] = sr
        cim_scr[lg] = si

        def pass2(a, h):
            hr, hi = h
            nr = ar * hr - ai * hi + sre_scr[lg, rows(a), :]
            ni = ar * hi + ai * hr + sim_scr[lg, rows(a), :]
            sre_scr[lg, rows(a), :] = nr
            sim_scr[lg, rows(a), :] = ni
            return nr, ni

        start = (jnp.concatenate(starts_r, axis=0), jnp.concatenate(starts_i, axis=0))
        lax.fori_loop(0, na, pass2, start, unroll=4)
        return carry

    lax.fori_loop(0, SSM_NLG, lane_group, 0)

    for lg in range(SSM_NLG):
        hre_ref[0, :, lg * SSM_LW:(lg + 1) * SSM_LW] = cre_scr[lg]
        him_ref[0, :, lg * SSM_LW:(lg + 1) * SSM_LW] = cim_scr[lg]

    ys = []
    per = half // SSM_LW
    for j in range(SSM_BLK):
        parts = [sre_scr[j * per + w] for w in range(per)] + [sim_scr[j * per + w] for w in range(per)]
        hcat = jnp.concatenate(parts, axis=1).astype(BF16)
        ys.append(_dot(hcat, wc_ref[j]))
    y = jnp.concatenate(ys, axis=1)
    res = _glu_ln(u, y, d_ref, wglu_ref, g_ref, b_ref)
    for a in range(na):
        for c in range(ncol):
            out_ref[pl.ds(a * ncol + c, SUBLANES, stride=ncol * na), :] = (
                res[a * SUBLANES:(a + 1) * SUBLANES, c * LANES:(c + 1) * LANES])


def _ssm_scan_call(u, abr, abi, anr, ani, wb, wc, d_skip, wglu, g, b, *, batch, seq):
    nchunk = seq // SSM_TC
    ncol = D_MODEL // LANES
    row = pl.BlockSpec((SSM_TC * ncol, LANES), lambda bb, c: (bb * nchunk + c, 0))
    vec = pl.BlockSpec((1, D_MODEL), lambda bb, c: (0, 0))
    lanes3 = pl.BlockSpec((SSM_NLG, 1, SSM_LW), lambda bb, c: (0, 0, 0))
    state = pl.BlockSpec((1, 1, N_STATES), lambda bb, c: (bb, 0, 0))
    sshape = jax.ShapeDtypeStruct((batch, 1, N_STATES), F32)
    g3 = lambda a: a.reshape(SSM_NLG, 1, SSM_LW)
    out, sre, sim = pl.pallas_call(
        _ssm_scan_kernel,
        grid=(batch, nchunk),
        in_specs=[row, lanes3, lanes3, lanes3, lanes3,
                  pl.BlockSpec(wb.shape, lambda bb, c: (0, 0, 0)),
                  pl.BlockSpec(wc.shape, lambda bb, c: (0, 0, 0)),
                  vec, pl.BlockSpec((D_MODEL, 2 * D_MODEL), lambda bb, c: (0, 0)), vec, vec],
        out_specs=[row, state, state],
        out_shape=[jax.ShapeDtypeStruct((batch * seq * ncol, LANES), F32), sshape, sshape],
        scratch_shapes=[pltpu.VMEM((SSM_TC, D_MODEL), F32),
                        pltpu.VMEM((SSM_NLG, SSM_TC, SSM_LW), F32),
                        pltpu.VMEM((SSM_NLG, SSM_TC, SSM_LW), F32),
                        pltpu.VMEM((SSM_NLG, 1, SSM_LW), F32),
                        pltpu.VMEM((SSM_NLG, 1, SSM_LW), F32)],
        compiler_params=_cparams("parallel", "arbitrary"),
        name="ssm_scan",
    )(u.reshape(batch * seq * ncol, LANES), g3(abr), g3(abi), g3(anr), g3(ani), wb, wc, d_skip, wglu, g, b)
    return out.reshape(batch * seq, D_MODEL), sre, sim


def _ssm_step_kernel(u_ref, h0r_ref, h0i_ref, abr_ref, abi_ref, wb_ref, wc_ref, d_ref, wglu_ref, g_ref, b_ref,
                     out_ref, hre_ref, him_ref):
    u = u_ref[...]
    half = N_STATES // SSM_BLK
    abr, abi = abr_ref[...], abi_ref[...]
    h0r, h0i = h0r_ref[...], h0i_ref[...]
    ys = []
    for j in range(SSM_BLK):
        bu = jnp.dot(u[:, j * MXU_DIM:(j + 1) * MXU_DIM], wb_ref[j], preferred_element_type=F32,
                     precision=lax.Precision.HIGHEST)
        cols = slice(j * half, (j + 1) * half)
        hr = abr[:, cols] * h0r[:, cols] - abi[:, cols] * h0i[:, cols] + bu[:, :half]
        hi = abr[:, cols] * h0i[:, cols] + abi[:, cols] * h0r[:, cols] + bu[:, half:]
        hre_ref[:, cols] = hr
        him_ref[:, cols] = hi
        hcat = jnp.concatenate([hr, hi], axis=1).astype(BF16)
        ys.append(_dot(hcat, wc_ref[j]))
    y = jnp.concatenate(ys, axis=1)
    out_ref[...] = _glu_ln(u, y, d_ref, wglu_ref, g_ref, b_ref)


def _ssm_step_call(u, h0r, h0i, abr, abi, wb_f32, wc, d_skip, wglu, g, b):
    n = u.shape[0]
    sshape = jax.ShapeDtypeStruct((n, N_STATES), F32)
    return pl.pallas_call(
        _ssm_step_kernel,
        out_shape=[jax.ShapeDtypeStruct((n, D_MODEL), F32), sshape, sshape],
        compiler_params=pltpu.CompilerParams(vmem_limit_bytes=VMEM_LIMIT),
        name="ssm_step",
    )(u, h0r, h0i, abr, abi, wb_f32, wc, d_skip, wglu, g, b)


def kernel(x_prompt, x_sample, cache_k, cache_v, state_ssm_re, state_ssm_im, page_table, w_qkv, w_lambda,
           attn_subln, w_attn_out, ssm_a_re, ssm_a_im, ssm_log_step, ssm_b_re, ssm_b_im, ssm_c_re, ssm_c_im,
           ssm_d, w_glu, w_router, router_bias, w_expert_in, w_expert_out, ln_gain, ln_bias):
    bp, lp, _ = x_prompt.shape
    bs, ls, _ = x_sample.shape
    assert ls == 1 and lp % SSM_TC == 0 and lp % ATT_TQ == 0
    n_pages = page_table.shape[1]
    past_len = n_pages * PAGE_SIZE
    n_pool = cache_k.shape[1]
    tp = bp * lp
    xp = x_prompt.reshape(tp, D_MODEL)
    xs = x_sample.reshape(bs, D_MODEL)

    vec = lambda a: a.reshape(1, D_MODEL)
    wr_pad = jnp.pad(w_router, ((0, 0), (0, LANES - N_EXPERTS)))
    rb_col = router_bias.reshape(N_EXPERTS, 1)

    def moe(x, layer, tm):
        win = w_expert_in[layer].astype(BF16)
        wout = w_expert_out[layer].reshape(N_EXPERTS * D_EXPERT, D_MODEL).astype(BF16)
        n = x.shape[0]
        pad = -n % tm
        xpad = jnp.pad(x, ((0, pad), (0, 0))) if pad else x
        out = _moe_ln_call(xpad, wr_pad, rb_col, win, wout, vec(ln_gain[layer, 1]), vec(ln_bias[layer, 1]), tm=tm)
        return out[:n] if pad else out

    lam_init = 0.8 - 0.6 * math.exp(-0.3 * 0)
    wqkv = w_qkv[0].astype(BF16)
    wo = w_attn_out[0].astype(BF16)
    subln = attn_subln[0].reshape(1, V_DIM)
    tab_p = _rope_tables(jnp.arange(lp, dtype=jnp.int32))
    tab_s = _rope_tables(jnp.broadcast_to(past_len + jnp.arange(ls, dtype=jnp.int32), (bs,)))
    k_p, v_p, qm, khm, vhm = _qkv_call(xp, wqkv, tab_p, tm=QKV_TM, table_blocks=lp // QKV_TM, head_major=True)
    k_s, v_s, q_s = _qkv_call(xs, wqkv, tab_s, tm=bs, table_blocks=1, head_major=False)
    o_p = _attn_call(qm, khm, vhm, w_lambda[0], subln, batch=bp, seq=lp, lam_init=lam_init)
    ck = cache_k[0].transpose(0, 2, 3, 4, 1).reshape(1, n_pool, D_MODEL, PAGE_SIZE)
    cv = cache_v[0].reshape(1, n_pool, PAGE_SIZE * N_HEADS, V_DIM)
    o_s = _dec_attn_call(page_table, q_s, k_s, v_s, ck, cv, w_lambda[0], subln, lam_init=lam_init)
    g0, b0 = vec(ln_gain[0, 0]), vec(ln_bias[0, 0])
    hp = _proj_ln_call(o_p, wo, xp, g0, b0, tm=ROW_TM)
    hs = _proj_ln_call(o_s, wo, xs, g0, b0, tm=bs)
    hp = moe(hp, 0, MOE_TM)
    hs = moe(hs, 0, LANES)

    abr, abi, anr, ani, wre, wim = _ssm_param_call(ssm_a_re[0], ssm_a_im[0], ssm_log_step[0],
                                                   ssm_b_re[0], ssm_b_im[0])
    lane_row = lambda a: a.reshape(1, N_STATES)
    wb = jnp.concatenate([_block_diag_in(wre), _block_diag_in(wim)], axis=-1)
    wc = jnp.concatenate([_block_diag_out(ssm_c_re[0]), -_block_diag_out(ssm_c_im[0])], axis=1)
    wc = wc.astype(BF16)
    wglu = w_glu[0].astype(BF16)
    d_skip = vec(ssm_d[0])
    g1, b1 = vec(ln_gain[1, 0]), vec(ln_bias[1, 0])
    hp, sre_p, sim_p = _ssm_scan_call(hp, abr, abi, anr, ani, wb.astype(BF16), wc, d_skip, wglu, g1, b1,
                                      batch=bp, seq=lp)
    hs, sre_s, sim_s = _ssm_step_call(hs, state_ssm_re[0].reshape(bs, N_STATES),
                                      state_ssm_im[0].reshape(bs, N_STATES),
                                      lane_row(abr), lane_row(abi), wb, wc, d_skip, wglu, g1, b1)
    hp = moe(hp, 1, MOE_TM)
    hs = moe(hs, 1, LANES)

    st = lambda a, n: a.reshape(1, n, N_SSM_GROUPS, SSM_STATE)
    return (hp.reshape(bp, lp, D_MODEL), hs.reshape(bs, ls, D_MODEL),
            k_p.reshape(1, bp, lp, N_HEADS, 2, HEAD_DIM), v_p.reshape(1, bp, lp, N_HEADS, V_DIM),
            st(sre_p, bp), st(sim_p, bp),
            k_s.reshape(1, bs, ls, N_HEADS, 2, HEAD_DIM), v_s.reshape(1, bs, ls, N_HEADS, V_DIM),
            st(sre_s, bs), st(sim_s, bs))
```

```python
import functools
import math

import jax
import jax.numpy as jnp
from jax import lax
from jax.experimental import pallas as pl
from jax.experimental.pallas import tpu as pltpu

F32 = jnp.float32
BF16 = jnp.bfloat16

D_MODEL = 1024
DEPTH = 2
PAGE_SIZE = 128
N_HEADS = 8
HEAD_DIM = 64
V_DIM = 2 * HEAD_DIM
ROT_DIM = HEAD_DIM // 4
ROPE_THETA = 500000.0
SSM_GROUP = 16
N_SSM_GROUPS = D_MODEL // SSM_GROUP
SSM_STATE = 64
N_STATES = N_SSM_GROUPS * SSM_STATE
N_EXPERTS = 16
N_EXPERT_GROUPS = 4
EXPERTS_PER_GROUP = N_EXPERTS // N_EXPERT_GROUPS
D_EXPERT = 256
ALPHA = (2 * DEPTH) ** 0.25
LN_EPS = 1e-5
QK_SCALE = HEAD_DIM ** -0.5

LANES = 128
SUBLANES = 8
MXU_DIM = 256
VMEM_LIMIT = 56 * 1024 * 1024

QKV_TM = 256
ATT_TQ = 512
ROW_TM = 512
MOE_TM = 256
SSM_TC = 256
SSM_NA = SSM_TC // SUBLANES
SSM_LW = 512
SSM_NLG = N_STATES // SSM_LW
SSM_BLK = 4
DEC_NP = 8


def _cparams(*sem):
    return pltpu.CompilerParams(dimension_semantics=sem, vmem_limit_bytes=VMEM_LIMIT)


def _layer_norm(z, g, b):
    mu = jnp.mean(z, axis=-1, keepdims=True)
    d = z - mu
    var = jnp.mean(d * d, axis=-1, keepdims=True)
    return d * lax.rsqrt(var + LN_EPS) * g + b


def _dot(a, b):
    return jnp.dot(a, b, preferred_element_type=F32)


def _dot_nt(a, b):
    return lax.dot_general(a, b, (((1,), (1,)), ((), ())), preferred_element_type=F32)


def _rope(x, cos, s_up, s_dn):
    outs = []
    for h in range(N_HEADS):
        xh = x[:, h * V_DIM:(h + 1) * V_DIM]
        up = pltpu.roll(xh, V_DIM - ROT_DIM // 2, 1)
        dn = pltpu.roll(xh, ROT_DIM // 2, 1)
        outs.append(xh * cos + up * s_up + dn * s_dn)
    return outs


def _qkv_kernel(x_ref, w_ref, cos_ref, sup_ref, sdn_ref, *out_refs, prompt):
    y = _dot(x_ref[...].astype(BF16), w_ref[...])
    cos, s_up, s_dn = cos_ref[...], sup_ref[...], sdn_ref[...]
    q = _rope(y[:, :D_MODEL], cos, s_up, s_dn)
    k = _rope(y[:, D_MODEL:2 * D_MODEL], cos, s_up, s_dn)
    v = y[:, 2 * D_MODEL:]
    if prompt:
        kt_ref, v_ref, qm_ref, khm_ref, vt_ref = out_refs
        lane = lax.broadcasted_iota(jnp.int32, (1, V_DIM), 1)
        first = lane < HEAD_DIM
        for h in range(N_HEADS):
            cols = slice(h * V_DIM, (h + 1) * V_DIM)
            qh = q[h] * QK_SCALE
            qm_ref[0, h] = jnp.where(first, qh, 0.0).astype(BF16)
            qm_ref[1, h] = jnp.where(first, 0.0, qh).astype(BF16)
            khm_ref[h] = k[h].astype(BF16)
            kt_ref[0, cols, :] = k[h].T
            vt_ref[h, 0] = v[:, cols].T.astype(BF16)
    else:
        k_ref, v_ref, q_ref = out_refs
        for h in range(N_HEADS):
            cols = slice(h * V_DIM, (h + 1) * V_DIM)
            q_ref[:, cols] = q[h] * QK_SCALE
            k_ref[:, cols] = k[h]
    v_ref[...] = v


def _qkv_call(x, w_bf16, tables, *, tm, seq, prompt):
    t = x.shape[0]
    nblk = seq // tm
    row = pl.BlockSpec((tm, D_MODEL), lambda i: (i, 0))
    tab = pl.BlockSpec((tm, V_DIM), lambda i: (i % nblk, 0))
    rows_f32 = jax.ShapeDtypeStruct((t, D_MODEL), F32)
    if prompt:
        hm = pl.BlockSpec((N_HEADS, tm, V_DIM), lambda i: (0, i, 0))
        out_shape = [jax.ShapeDtypeStruct((t // seq, D_MODEL, seq), F32), rows_f32,
                     jax.ShapeDtypeStruct((2, N_HEADS, t, V_DIM), BF16),
                     jax.ShapeDtypeStruct((N_HEADS, t, V_DIM), BF16),
                     jax.ShapeDtypeStruct((N_HEADS, t // tm, V_DIM, tm), BF16)]
        out_specs = [pl.BlockSpec((1, D_MODEL, tm), lambda i: (i // nblk, 0, i % nblk)), row,
                     pl.BlockSpec((2, N_HEADS, tm, V_DIM), lambda i: (0, 0, i, 0)), hm,
                     pl.BlockSpec((N_HEADS, 1, V_DIM, tm), lambda i: (0, i, 0, 0))]
    else:
        out_shape = [rows_f32] * 3
        out_specs = [row] * 3
    return pl.pallas_call(
        functools.partial(_qkv_kernel, prompt=prompt),
        grid=(t // tm,),
        in_specs=[row, pl.BlockSpec((D_MODEL, 3 * D_MODEL), lambda i: (0, 0)), tab, tab, tab],
        out_specs=out_specs,
        out_shape=out_shape,
        compiler_params=_cparams("parallel"),
        name="qkv_rope",
    )(x, w_bf16, *tables)


def _rope_tables(pos):
    half = ROT_DIM // 2
    inv = ROPE_THETA ** (-jnp.arange(half, dtype=F32) * 2.0 / ROT_DIM)
    ang = pos.astype(F32)[:, None] * inv
    cos, sin = jnp.cos(ang), jnp.sin(ang)
    n = pos.shape[0]
    ones = jnp.ones((n, HEAD_DIM - ROT_DIM), F32)
    zeros = jnp.zeros((n, HEAD_DIM - ROT_DIM), F32)
    z8 = jnp.zeros((n, half), F32)
    c = jnp.concatenate([cos, cos, ones], axis=1)
    s_up = jnp.concatenate([-sin, z8, zeros], axis=1)
    s_dn = jnp.concatenate([z8, sin, zeros], axis=1)
    return tuple(jnp.tile(a, (1, 2)) for a in (c, s_up, s_dn))


def _diff_lambda(wl, lam_init):
    a = jnp.sum(wl[0:1] * wl[1:2], axis=-1, keepdims=True)
    b = jnp.sum(wl[2:3] * wl[3:4], axis=-1, keepdims=True)
    return jnp.exp(a) - jnp.exp(b) + lam_init


def _sub_norm(o, g, lam_init):
    o = o * lax.rsqrt(jnp.mean(o * o, axis=-1, keepdims=True) + LN_EPS)
    return o * g * (1.0 - lam_init)


def _softmax_step(s, m_ref, l_ref):
    m_prev = m_ref[...]
    m_new = jnp.maximum(m_prev, jnp.max(s, axis=-1, keepdims=True))
    alpha = jnp.exp(m_prev - m_new)
    p = jnp.exp(s - m_new[:, :1])
    l_ref[...] = alpha * l_ref[...] + jnp.sum(p, axis=-1, keepdims=True)
    m_ref[...] = m_new
    return alpha, p


def _attn_kernel(wl_ref, g_ref, q_ref, k_ref, vt_ref, o_ref, acc_ref, *, tq, kb, lam_init):
    qi = pl.program_id(2)
    q = q_ref[...].reshape(2 * tq, V_DIM)
    acc_ref[...] = jnp.zeros(acc_ref.shape, F32)
    nkb = tq // kb

    def tile(j, carry, masked):
        m, l = carry
        k = k_ref[0, pl.ds(pl.multiple_of(j * tq, tq), tq), :]
        st = _dot_nt(k, q)
        if masked:
            key = lax.broadcasted_iota(jnp.int32, st.shape, 0)
            qry = lax.broadcasted_iota(jnp.int32, st.shape, 1)
            qry = jnp.where(qry >= tq, qry - tq, qry)
            st = jnp.where(key <= qry, st, -jnp.inf)
        m_new = jnp.maximum(m, jnp.max(st, axis=0, keepdims=True))
        alpha = jnp.exp(m - m_new)
        p = jnp.exp(st - m_new)
        l = alpha * l + jnp.sum(p, axis=0, keepdims=True)
        vt = jnp.concatenate([vt_ref[0, j * nkb + b] for b in range(nkb)], axis=1)
        acc_ref[...] = alpha * acc_ref[...] + _dot(vt, p.astype(BF16))
        return m_new, l

    init = (jnp.full((1, 2 * tq), -jnp.inf, F32), jnp.zeros((1, 2 * tq), F32))
    carry = lax.fori_loop(0, qi, lambda j, c: tile(j, c, False), init)
    _, l = tile(qi, carry, True)

    out_t = acc_ref[...] / l
    lam = _diff_lambda(wl_ref[...], lam_init)
    o = (out_t[:, :tq] - lam * out_t[:, tq:]).T
    o_ref[...] = _sub_norm(o, g_ref[...], lam_init).astype(o_ref.dtype)


def _attn_call(qm, khm, vt, w_lambda, subln, *, batch, seq, lam_init):
    tq = ATT_TQ
    nq = seq // tq
    kb = vt.shape[-1]
    return pl.pallas_call(
        functools.partial(_attn_kernel, tq=tq, kb=kb, lam_init=lam_init),
        grid=(batch, N_HEADS, nq),
        in_specs=[pl.BlockSpec((4, HEAD_DIM), lambda b, h, i: (0, 0)),
                  pl.BlockSpec((1, V_DIM), lambda b, h, i: (0, 0)),
                  pl.BlockSpec((2, 1, tq, V_DIM), lambda b, h, i: (0, h, b * nq + i, 0)),
                  pl.BlockSpec((1, seq, V_DIM), lambda b, h, i: (h, b, 0)),
                  pl.BlockSpec((1, seq // kb, V_DIM, kb), lambda b, h, i: (h, b, 0, 0))],
        out_specs=pl.BlockSpec((tq, V_DIM), lambda b, h, i: (b * nq + i, h)),
        out_shape=jax.ShapeDtypeStruct((batch * seq, D_MODEL), BF16),
        scratch_shapes=[pltpu.VMEM((V_DIM, 2 * tq), F32)],
        compiler_params=_cparams("parallel", "parallel", "arbitrary"),
        name="attn_prompt",
    )(w_lambda, subln, qm, khm, vt)


def _dec_attn_kernel(pt_ref, wl_ref, g_ref, q_ref, kn_ref, vn_ref, e_ref, *rest, n_pages, lam_init):
    k_refs = rest[:DEC_NP]
    v_refs = rest[DEC_NP:2 * DEC_NP]
    o_ref, qrow_ref, m_ref, l_ref, acc_ref = rest[2 * DEC_NP:]
    step_id = pl.program_id(1)
    n_rows = 2 * N_HEADS

    @pl.when(step_id == 0)
    def _():
        m_ref[...] = jnp.full(m_ref.shape, -jnp.inf, F32)
        l_ref[...] = jnp.zeros(l_ref.shape, F32)
        acc_ref[...] = jnp.zeros(acc_ref.shape, F32)
        row = lax.broadcasted_iota(jnp.int32, (n_rows, D_MODEL), 0)
        lane = lax.broadcasted_iota(jnp.int32, (n_rows, D_MODEL), 1)
        block = (row & (N_HEADS - 1)) * 2 + (row >> 3)
        qrow_ref[...] = jnp.where((lane >> 6) == block, q_ref[0], 0.0)

    qrows = qrow_ref[...]
    qb = qrows.astype(BF16)
    row = lax.broadcasted_iota(jnp.int32, (n_rows, PAGE_SIZE * N_HEADS), 0)
    lane = lax.broadcasted_iota(jnp.int32, (n_rows, PAGE_SIZE * N_HEADS), 1)
    own_head = (lane & (N_HEADS - 1)) == (row & (N_HEADS - 1))
    s = jnp.concatenate([_dot(qb, k_refs[i][0, 0].astype(BF16)) for i in range(DEC_NP)], axis=1)
    alpha, p = _softmax_step(s, m_ref, l_ref)
    pv = None
    for i in range(DEC_NP):
        pexp = _dot(p[:, i * PAGE_SIZE:(i + 1) * PAGE_SIZE].astype(BF16), e_ref[...])
        pexp = jnp.where(own_head, pexp, 0.0).astype(BF16)
        part = _dot(pexp, v_refs[i][0, 0].astype(BF16))
        pv = part if pv is None else pv + part
    acc_ref[...] = alpha * acc_ref[...] + pv

    @pl.when(step_id == n_pages // DEC_NP - 1)
    def _():
        s = jnp.sum(qrows * kn_ref[0], axis=-1, keepdims=True)
        m_prev = m_ref[...]
        m_new = jnp.maximum(m_prev, s)
        alpha = jnp.exp(m_prev - m_new)
        p = jnp.exp(s - m_new[:, :1])
        l = alpha * l_ref[...] + p
        vn = vn_ref[0]
        acc = alpha * acc_ref[...] + p * jnp.concatenate([vn, vn], axis=0)
        out = acc / l
        lam = _diff_lambda(wl_ref[...], lam_init)
        o = out[:N_HEADS] - lam * out[N_HEADS:]
        o_ref[0] = _sub_norm(o, g_ref[...], lam_init).astype(o_ref.dtype)


def _dec_attn_call(page_table, q, k_new, v_new, cache_k, cache_v, w_lambda, subln, *, lam_init):
    nseq, n_pages = page_table.shape
    row = pl.BlockSpec((1, 1, D_MODEL), lambda b, p, pt: (b, 0, 0))
    heads = pl.BlockSpec((1, N_HEADS, V_DIM), lambda b, p, pt: (b, 0, 0))

    def page_spec(i):
        return pl.BlockSpec((1, 1, D_MODEL, PAGE_SIZE), lambda b, p, pt: (0, pt[b, p * DEC_NP + i], 0, 0))

    pages = [page_spec(i) for i in range(DEC_NP)]
    expand = (jnp.arange(PAGE_SIZE * N_HEADS)[None, :] // N_HEADS == jnp.arange(PAGE_SIZE)[:, None]).astype(BF16)
    grid_spec = pltpu.PrefetchScalarGridSpec(
        num_scalar_prefetch=1,
        grid=(nseq, n_pages // DEC_NP),
        in_specs=[pl.BlockSpec((4, HEAD_DIM), lambda b, p, pt: (0, 0)),
                  pl.BlockSpec((1, V_DIM), lambda b, p, pt: (0, 0)),
                  row, row, heads,
                  pl.BlockSpec((PAGE_SIZE, PAGE_SIZE * N_HEADS), lambda b, p, pt: (0, 0))] + pages + pages,
        out_specs=heads,
        scratch_shapes=[pltpu.VMEM((2 * N_HEADS, D_MODEL), F32),
                        pltpu.VMEM((2 * N_HEADS, LANES), F32),
                        pltpu.VMEM((2 * N_HEADS, LANES), F32),
                        pltpu.VMEM((2 * N_HEADS, V_DIM), F32)],
    )
    r3 = lambda a: a.reshape(nseq, 1, D_MODEL)
    out = pl.pallas_call(
        functools.partial(_dec_attn_kernel, n_pages=n_pages, lam_init=lam_init),
        grid_spec=grid_spec,
        out_shape=jax.ShapeDtypeStruct((nseq, N_HEADS, V_DIM), BF16),
        compiler_params=_cparams("parallel", "arbitrary"),
        name="attn_decode",
    )(page_table, w_lambda, subln, r3(q), r3(k_new), v_new.reshape(nseq, N_HEADS, V_DIM), expand,
      *([cache_k] * DEC_NP), *([cache_v] * DEC_NP))
    return out.reshape(nseq, D_MODEL)


def _proj_ln_kernel(o_ref, w_ref, x_ref, g_ref, b_ref, out_ref):
    z = ALPHA * x_ref[...] + _dot(o_ref[...], w_ref[...])
    out_ref[...] = _layer_norm(z, g_ref[...], b_ref[...])


def _proj_ln_call(o, w_bf16, x, g, b, *, tm):
    t = x.shape[0]
    row = pl.BlockSpec((tm, D_MODEL), lambda i: (i, 0))
    vec = pl.BlockSpec((1, D_MODEL), lambda i: (0, 0))
    return pl.pallas_call(
        _proj_ln_kernel,
        grid=(t // tm,),
        in_specs=[row, pl.BlockSpec((D_MODEL, D_MODEL), lambda i: (0, 0)), row, vec, vec],
        out_specs=row,
        out_shape=jax.ShapeDtypeStruct((t, D_MODEL), F32),
        compiler_params=_cparams("parallel"),
        name="proj_ln",
    )(o, w_bf16, x, g, b)


def _route(scores, sel):
    def pair_max(vals):
        best = None
        for i in range(len(vals)):
            for j in range(i + 1, len(vals)):
                s = vals[i] + vals[j]
                best = s if best is None else jnp.maximum(best, s)
        return best

    grp = [sel[g * EXPERTS_PER_GROUP:(g + 1) * EXPERTS_PER_GROUP] for g in range(N_EXPERT_GROUPS)]
    grp_score = [pair_max(v) for v in grp]
    best, g_idx = grp_score[0], jnp.zeros_like(grp_score[0], dtype=jnp.int32)
    for g in range(1, N_EXPERT_GROUPS):
        upd = grp_score[g] > best
        g_idx = jnp.where(upd, g, g_idx)
        best = jnp.where(upd, grp_score[g], best)

    def pick(rows, j):
        out = rows[j]
        for g in range(1, N_EXPERT_GROUPS):
            out = jnp.where(g_idx == g, rows[g * EXPERTS_PER_GROUP + j], out)
        return out

    in_sel = [pick(sel, j) for j in range(EXPERTS_PER_GROUP)]
    in_sc = [pick(scores, j) for j in range(EXPERTS_PER_GROUP)]

    def arg_first_max(vals, excluded):
        bv, bi = None, None
        for j, v in enumerate(vals):
            v = v if excluded is None else jnp.where(excluded == j, -jnp.inf, v)
            if bv is None:
                bv, bi = v, jnp.zeros_like(v, dtype=jnp.int32)
            else:
                upd = v > bv
                bi = jnp.where(upd, j, bi)
                bv = jnp.where(upd, v, bv)
        return bi

    l1 = arg_first_max(in_sel, None)
    l2 = arg_first_max(in_sel, l1)

    def take(vals, idx):
        out = vals[0]
        for j in range(1, len(vals)):
            out = jnp.where(idx == j, vals[j], out)
        return out

    w1, w2 = take(in_sc, l1), take(in_sc, l2)
    tot = w1 + w2
    w1, w2 = w1 / tot, w2 / tot
    gates = []
    for e in range(N_EXPERTS):
        g, j = divmod(e, EXPERTS_PER_GROUP)
        local = jnp.where(l1 == j, w1, jnp.where(l2 == j, w2, 0.0))
        gates.append(jnp.where(g_idx == g, local, 0.0))
    return gates


def _moe_ln_kernel(x_ref, wr_ref, rb_ref, win_ref, wout_ref, g_ref, b_ref, out_ref, hcat_ref):
    x = x_ref[...]
    tm = x.shape[0]
    logits = jnp.dot(x, wr_ref[...], preferred_element_type=F32, precision=lax.Precision.HIGHEST)
    scores_t = jax.nn.sigmoid(logits.T[:N_EXPERTS])
    sel_t = scores_t + rb_ref[...]
    rows = lambda a: [a[e:e + 1] for e in range(N_EXPERTS)]
    gates = _route(rows(scores_t), rows(sel_t))
    gates_t = jnp.concatenate(gates + [jnp.zeros((LANES - N_EXPERTS, tm), F32)], axis=0)
    gate_cols = gates_t.T
    xb = x.astype(BF16)
    for e in range(N_EXPERTS):
        h = _dot(xb, win_ref[e])
        a = jax.nn.silu(h[:, :D_EXPERT]) * h[:, D_EXPERT:] * gate_cols[:, e:e + 1]
        hcat_ref[:, e * D_EXPERT:(e + 1) * D_EXPERT] = a.astype(BF16)
    z = ALPHA * x + _dot(hcat_ref[...], wout_ref[...])
    out_ref[...] = _layer_norm(z, g_ref[...], b_ref[...])


def _moe_ln_call(x, wr_pad, rb_col, win_bf16, wout_bf16, g, b, *, tm):
    t = x.shape[0]
    row = pl.BlockSpec((tm, D_MODEL), lambda i: (i, 0))
    vec = pl.BlockSpec((1, D_MODEL), lambda i: (0, 0))
    once = pl.Buffered(1)
    return pl.pallas_call(
        _moe_ln_kernel,
        grid=(t // tm,),
        in_specs=[row,
                  pl.BlockSpec((D_MODEL, LANES), lambda i: (0, 0)),
                  pl.BlockSpec((N_EXPERTS, 1), lambda i: (0, 0)),
                  pl.BlockSpec((N_EXPERTS, D_MODEL, 2 * D_EXPERT), lambda i: (0, 0, 0), pipeline_mode=once),
                  pl.BlockSpec((N_EXPERTS * D_EXPERT, D_MODEL), lambda i: (0, 0), pipeline_mode=once),
                  vec, vec],
        out_specs=row,
        out_shape=jax.ShapeDtypeStruct((t, D_MODEL), F32),
        scratch_shapes=[pltpu.VMEM((tm, N_EXPERTS * D_EXPERT), BF16)],
        compiler_params=_cparams("parallel"),
        name="moe_ln",
    )(x, wr_pad, rb_col, win_bf16, wout_bf16, g, b)


def _ssm_param_kernel(ar_ref, ai_ref, ls_ref, bre_ref, bim_ref,
                      abr_ref, abi_ref, anr_ref, ani_ref, wre_ref, wim_ref, *, n_sq):
    ar, ai = ar_ref[...], ai_ref[...]
    dt = jnp.exp(ls_ref[...])
    mag = jnp.exp(ar * dt)
    abr, abi = mag * jnp.cos(ai * dt), mag * jnp.sin(ai * dt)
    den = ar * ar + ai * ai
    cr = ((abr - 1.0) * ar + abi * ai) / den
    ci = (abi * ar - (abr - 1.0) * ai) / den
    bre, bim = bre_ref[...], bim_ref[...]
    wre_ref[...] = cr * bre - ci * bim
    wim_ref[...] = cr * bim + ci * bre
    abr_ref[...] = abr
    abi_ref[...] = abi
    pr, pi = abr, abi
    for _ in range(n_sq):
        pr, pi = pr * pr - pi * pi, 2.0 * pr * pi
    anr_ref[...] = pr
    ani_ref[...] = pi


def _ssm_param_call(a_re, a_im, log_step, b_re, b_im):
    col = lambda a: a.reshape(N_STATES, 1)
    ls = jnp.broadcast_to(log_step[:, None], (N_SSM_GROUPS, SSM_STATE))
    wide = lambda a: a.reshape(N_STATES, SSM_GROUP)
    cshape = jax.ShapeDtypeStruct((N_STATES, 1), F32)
    wshape = jax.ShapeDtypeStruct((N_STATES, SSM_GROUP), F32)
    return pl.pallas_call(
        functools.partial(_ssm_param_kernel, n_sq=int(math.log2(SSM_NA))),
        out_shape=[cshape] * 4 + [wshape] * 2,
        compiler_params=pltpu.CompilerParams(vmem_limit_bytes=VMEM_LIMIT),
        name="ssm_params",
    )(col(a_re), col(a_im), col(ls), wide(b_re), wide(b_im))


def _block_diag_in(w):
    gl = N_SSM_GROUPS // SSM_BLK
    w = w.reshape(SSM_BLK, gl, SSM_STATE, SSM_GROUP)
    eye = jnp.eye(gl, dtype=w.dtype)
    full = w.transpose(0, 1, 3, 2)[:, :, :, None, :] * eye[None, :, None, :, None]
    return full.reshape(SSM_BLK, gl * SSM_GROUP, gl * SSM_STATE)


def _block_diag_out(c):
    gl = N_SSM_GROUPS // SSM_BLK
    c = c.reshape(SSM_BLK, gl, SSM_GROUP, SSM_STATE)
    eye = jnp.eye(gl, dtype=c.dtype)
    full = c.transpose(0, 1, 3, 2)[:, :, :, None, :] * eye[None, :, None, :, None]
    return full.reshape(SSM_BLK, gl * SSM_STATE, gl * SSM_GROUP)


def _glu_ln(u, y, d_ref, wglu_ref, g_ref, b_ref):
    y = y + d_ref[...] * u
    gl = jax.nn.gelu(y).astype(BF16)
    vg = _dot(gl, wglu_ref[...])
    mix = vg[:, :D_MODEL] * jax.nn.sigmoid(vg[:, D_MODEL:])
    return _layer_norm(ALPHA * u + mix, g_ref[...], b_ref[...])


def _ssm_scan_kernel(u_ref, abr_ref, abi_ref, anr_ref, ani_ref, wb_ref, wc_ref, d_ref, wglu_ref, g_ref, b_ref,
                     out_ref, hre_ref, him_ref, u_scr, sre_scr, sim_scr, cre_scr, cim_scr):
    chunk = pl.program_id(1)
    na, tc = SSM_NA, SSM_TC

    @pl.when(chunk == 0)
    def _():
        cre_scr[...] = jnp.zeros(cre_scr.shape, F32)
        cim_scr[...] = jnp.zeros(cim_scr.shape, F32)

    ncol = D_MODEL // LANES
    for a in range(na):
        for c in range(ncol):
            u_scr[a * SUBLANES:(a + 1) * SUBLANES, c * LANES:(c + 1) * LANES] = (
                u_ref[pl.ds(a * ncol + c, SUBLANES, stride=ncol * na), :])
    u = u_scr[...]
    ub = u.astype(BF16)
    half = N_STATES // SSM_BLK
    for j in range(SSM_BLK):
        bu = _dot(ub[:, j * MXU_DIM:(j + 1) * MXU_DIM], wb_ref[j])
        for w in range(half // SSM_LW):
            lg = j * (half // SSM_LW) + w
            sre_scr[lg] = bu[:, w * SSM_LW:(w + 1) * SSM_LW]
            sim_scr[lg] = bu[:, half + w * SSM_LW:half + (w + 1) * SSM_LW]

    def lane_group(lg, carry):
        bshape = (SUBLANES, SSM_LW)
        ar = jnp.broadcast_to(abr_ref[lg], bshape)
        ai = jnp.broadcast_to(abi_ref[lg], bshape)

        def rows(a):
            return pl.ds(pl.multiple_of(a * SUBLANES, SUBLANES), SUBLANES)

        def pass1(a, h):
            hr, hi = h
            return (ar * hr - ai * hi + sre_scr[lg, rows(a), :], ar * hi + ai * hr + sim_scr[lg, rows(a), :])

        zero = jnp.zeros(bshape, F32)
        er, ei = lax.fori_loop(0, na, pass1, (zero, zero), unroll=4)

        anr, ani = anr_ref[lg], ani_ref[lg]
        sr, si = cre_scr[lg], cim_scr[lg]
        starts_r, starts_i = [], []
        for r in range(SUBLANES):
            starts_r.append(sr)
            starts_i.append(si)
            sr, si = (anr * sr - ani * si + er[r:r + 1], anr * si + ani * sr + ei[r:r + 1])
        cre_scr[lg] = sr
        cim_scr[lg] = si

        def pass2(a, h):
            hr, hi = h
            nr = ar * hr - ai * hi + sre_scr[lg, rows(a), :]
            ni = ar * hi + ai * hr + sim_scr[lg, rows(a), :]
            sre_scr[lg, rows(a), :] = nr
            sim_scr[lg, rows(a), :] = ni
            return nr, ni

        start = (jnp.concatenate(starts_r, axis=0), jnp.concatenate(starts_i, axis=0))
        lax.fori_loop(0, na, pass2, start, unroll=4)
        return carry

    lax.fori_loop(0, SSM_NLG, lane_group, 0)

    for lg in range(SSM_NLG):
        hre_ref[0, :, lg * SSM_LW:(lg + 1) * SSM_LW] = cre_scr[lg]
        him_ref[0, :, lg * SSM_LW:(lg + 1) * SSM_LW] = cim_scr[lg]

    ys = []
    per = half // SSM_LW
    for j in range(SSM_BLK):
        parts = [sre_scr[j * per + w] for w in range(per)] + [sim_scr[j * per + w] for w in range(per)]
        hcat = jnp.concatenate(parts, axis=1).astype(BF16)
        ys.append(_dot(hcat, wc_ref[j]))
    y = jnp.concatenate(ys, axis=1)
    res = _glu_ln(u, y, d_ref, wglu_ref, g_ref, b_ref)
    for a in range(na):
        for c in range(ncol):
            out_ref[pl.ds(a * ncol + c, SUBLANES, stride=ncol * na), :] = (
                res[a * SUBLANES:(a + 1) * SUBLANES, c * LANES:(c + 1) * LANES])


def _ssm_scan_call(u, abr, abi, anr, ani, wb, wc, d_skip, wglu, g, b, *, batch, seq):
    nchunk = seq // SSM_TC
    ncol = D_MODEL // LANES
    row = pl.BlockSpec((SSM_TC * ncol, LANES), lambda bb, c: (bb * nchunk + c, 0))
    vec = pl.BlockSpec((1, D_MODEL), lambda bb, c: (0, 0))
    lanes3 = pl.BlockSpec((SSM_NLG, 1, SSM_LW), lambda bb, c: (0, 0, 0))
    state = pl.BlockSpec((1, 1, N_STATES), lambda bb, c: (bb, 0, 0))
    sshape = jax.ShapeDtypeStruct((batch, 1, N_STATES), F32)
    g3 = lambda a: a.reshape(SSM_NLG, 1, SSM_LW)
    out, sre, sim = pl.pallas_call(
        _ssm_scan_kernel,
        grid=(batch, nchunk),
        in_specs=[row, lanes3, lanes3, lanes3, lanes3,
                  pl.BlockSpec(wb.shape, lambda bb, c: (0, 0, 0)),
                  pl.BlockSpec(wc.shape, lambda bb, c: (0, 0, 0)),
                  vec, pl.BlockSpec((D_MODEL, 2 * D_MODEL), lambda bb, c: (0, 0)), vec, vec],
        out_specs=[row, state, state],
        out_shape=[jax.ShapeDtypeStruct((batch * seq * ncol, LANES), F32), sshape, sshape],
        scratch_shapes=[pltpu.VMEM((SSM_TC, D_MODEL), F32),
                        pltpu.VMEM((SSM_NLG, SSM_TC, SSM_LW), F32),
                        pltpu.VMEM((SSM_NLG, SSM_TC, SSM_LW), F32),
                        pltpu.VMEM((SSM_NLG, 1, SSM_LW), F32),
                        pltpu.VMEM((SSM_NLG, 1, SSM_LW), F32)],
        compiler_params=_cparams("parallel", "arbitrary"),
        name="ssm_scan",
    )(u.reshape(batch * seq * ncol, LANES), g3(abr), g3(abi), g3(anr), g3(ani), wb, wc, d_skip, wglu, g, b)
    return out.reshape(batch * seq, D_MODEL), sre, sim


def _ssm_step_kernel(u_ref, h0r_ref, h0i_ref, abr_ref, abi_ref, wb_ref, wc_ref, d_ref, wglu_ref, g_ref, b_ref,
                     out_ref, hre_ref, him_ref):
    u = u_ref[...]
    half = N_STATES // SSM_BLK
    abr, abi = abr_ref[...], abi_ref[...]
    h0r, h0i = h0r_ref[...], h0i_ref[...]
    ys = []
    for j in range(SSM_BLK):
        bu = jnp.dot(u[:, j * MXU_DIM:(j + 1) * MXU_DIM], wb_ref[j], preferred_element_type=F32,
                     precision=lax.Precision.HIGHEST)
        cols = slice(j * half, (j + 1) * half)
        hr = abr[:, cols] * h0r[:, cols] - abi[:, cols] * h0i[:, cols] + bu[:, :half]
        hi = abr[:, cols] * h0i[:, cols] + abi[:, cols] * h0r[:, cols] + bu[:, half:]
        hre_ref[:, cols] = hr
        him_ref[:, cols] = hi
        hcat = jnp.concatenate([hr, hi], axis=1).astype(BF16)
        ys.append(_dot(hcat, wc_ref[j]))
    y = jnp.concatenate(ys, axis=1)
    out_ref[...] = _glu_ln(u, y, d_ref, wglu_ref, g_ref, b_ref)


def _ssm_step_call(u, h0r, h0i, abr, abi, wb_f32, wc, d_skip, wglu, g, b):
    n = u.shape[0]
    sshape = jax.ShapeDtypeStruct((n, N_STATES), F32)
    return pl.pallas_call(
        _ssm_step_kernel,
        out_shape=[jax.ShapeDtypeStruct((n, D_MODEL), F32), sshape, sshape],
        compiler_params=pltpu.CompilerParams(vmem_limit_bytes=VMEM_LIMIT),
        name="ssm_step",
    )(u, h0r, h0i, abr, abi, wb_f32, wc, d_skip, wglu, g, b)


def kernel(x_prompt, x_sample, cache_k, cache_v, state_ssm_re, state_ssm_im, page_table, w_qkv, w_lambda,
           attn_subln, w_attn_out, ssm_a_re, ssm_a_im, ssm_log_step, ssm_b_re, ssm_b_im, ssm_c_re, ssm_c_im,
           ssm_d, w_glu, w_router, router_bias, w_expert_in, w_expert_out, ln_gain, ln_bias):
    bp, lp, _ = x_prompt.shape
    bs, ls, _ = x_sample.shape
    assert ls == 1 and lp % SSM_TC == 0 and lp % ATT_TQ == 0
    n_pages = page_table.shape[1]
    past_len = n_pages * PAGE_SIZE
    n_pool = cache_k.shape[1]
    tp = bp * lp
    xp = x_prompt.reshape(tp, D_MODEL)
    xs = x_sample.reshape(bs, D_MODEL)

    vec = lambda a: a.reshape(1, D_MODEL)
    wr_pad = jnp.pad(w_router, ((0, 0), (0, LANES - N_EXPERTS)))
    rb_col = router_bias.reshape(N_EXPERTS, 1)

    def moe(x, layer, tm):
        win = w_expert_in[layer].astype(BF16)
        wout = w_expert_out[layer].reshape(N_EXPERTS * D_EXPERT, D_MODEL).astype(BF16)
        n = x.shape[0]
        pad = -n % tm
        xpad = jnp.pad(x, ((0, pad), (0, 0))) if pad else x
        out = _moe_ln_call(xpad, wr_pad, rb_col, win, wout, vec(ln_gain[layer, 1]), vec(ln_bias[layer, 1]), tm=tm)
        return out[:n] if pad else out

    lam_init = 0.8 - 0.6 * math.exp(-0.3 * 0)
    wqkv = w_qkv[0].astype(BF16)
    wo = w_attn_out[0].astype(BF16)
    subln = attn_subln[0].reshape(1, V_DIM)
    tab_p = _rope_tables(jnp.arange(lp, dtype=jnp.int32))
    tab_s = _rope_tables(jnp.broadcast_to(past_len + jnp.arange(ls, dtype=jnp.int32), (bs,)))
    kt_p, v_p, qm, khm, vt = _qkv_call(xp, wqkv, tab_p, tm=QKV_TM, seq=lp, prompt=True)
    k_s, v_s, q_s = _qkv_call(xs, wqkv, tab_s, tm=bs, seq=bs, prompt=False)
    o_p = _attn_call(qm, khm, vt, w_lambda[0], subln, batch=bp, seq=lp, lam_init=lam_init)
    ck = cache_k[0].transpose(0, 2, 3, 4, 1).reshape(1, n_pool, D_MODEL, PAGE_SIZE)
    cv = cache_v[0].reshape(1, n_pool, PAGE_SIZE * N_HEADS, V_DIM)
    o_s = _dec_attn_call(page_table, q_s, k_s, v_s, ck, cv, w_lambda[0], subln, lam_init=lam_init)
    g0, b0 = vec(ln_gain[0, 0]), vec(ln_bias[0, 0])
    hp = _proj_ln_call(o_p, wo, xp, g0, b0, tm=ROW_TM)
    hs = _proj_ln_call(o_s, wo, xs, g0, b0, tm=bs)
    hp = moe(hp, 0, MOE_TM)
    hs = moe(hs, 0, LANES)

    abr, abi, anr, ani, wre, wim = _ssm_param_call(ssm_a_re[0], ssm_a_im[0], ssm_log_step[0],
                                                   ssm_b_re[0], ssm_b_im[0])
    lane_row = lambda a: a.reshape(1, N_STATES)
    wb = jnp.concatenate([_block_diag_in(wre), _block_diag_in(wim)], axis=-1)
    wc = jnp.concatenate([_block_diag_out(ssm_c_re[0]), -_block_diag_out(ssm_c_im[0])], axis=1)
    wc = wc.astype(BF16)
    wglu = w_glu[0].astype(BF16)
    d_skip = vec(ssm_d[0])
    g1, b1 = vec(ln_gain[1, 0]), vec(ln_bias[1, 0])
    hp, sre_p, sim_p = _ssm_scan_call(hp, abr, abi, anr, ani, wb.astype(BF16), wc, d_skip, wglu, g1, b1,
                                      batch=bp, seq=lp)
    hs, sre_s, sim_s = _ssm_step_call(hs, state_ssm_re[0].reshape(bs, N_STATES),
                                      state_ssm_im[0].reshape(bs, N_STATES),
                                      lane_row(abr), lane_row(abi), wb, wc, d_skip, wglu, g1, b1)
    hp = moe(hp, 1, MOE_TM)
    hs = moe(hs, 1, LANES)

    st = lambda a, n: a.reshape(1, n, N_SSM_GROUPS, SSM_STATE)
    return (hp.reshape(bp, lp, D_MODEL), hs.reshape(bs, ls, D_MODEL),
            kt_p.reshape(1, bp, N_HEADS, 2, HEAD_DIM, lp).transpose(0, 1, 5, 2, 3, 4),
            v_p.reshape(1, bp, lp, N_HEADS, V_DIM),
            st(sre_p, bp), st(sim_p, bp),
            k_s.reshape(1, bs, ls, N_HEADS, 2, HEAD_DIM), v_s.reshape(1, bs, ls, N_HEADS, V_DIM),
            st(sre_s, bs), st(sim_s, bs))
```

```python
import functools
import math

import jax
import jax.numpy as jnp
from jax import lax
from jax.experimental import pallas as pl
from jax.experimental.pallas import tpu as pltpu

F32 = jnp.float32
BF16 = jnp.bfloat16

D_MODEL = 1024
DEPTH = 2
PAGE_SIZE = 128
N_HEADS = 8
HEAD_DIM = 64
V_DIM = 2 * HEAD_DIM
ROT_DIM = HEAD_DIM // 4
ROPE_THETA = 500000.0
SSM_GROUP = 16
N_SSM_GROUPS = D_MODEL // SSM_GROUP
SSM_STATE = 64
N_STATES = N_SSM_GROUPS * SSM_STATE
N_EXPERTS = 16
N_EXPERT_GROUPS = 4
EXPERTS_PER_GROUP = N_EXPERTS // N_EXPERT_GROUPS
D_EXPERT = 256
ALPHA = (2 * DEPTH) ** 0.25
LN_EPS = 1e-5
QK_SCALE = HEAD_DIM ** -0.5
LOG2E = math.log2(math.e)

LANES = 128
SUBLANES = 8
MXU_DIM = 256
VMEM_LIMIT = 56 * 1024 * 1024

QKV_TM = 256
ATT_TQ = 512
ROW_TM = 512
MOE_TM = 256
SSM_TC = 256
SSM_NA = SSM_TC // SUBLANES
SSM_LW = 512
SSM_NLG = N_STATES // SSM_LW
SSM_BLK = 4
DEC_NP = 8
ONES_ROWS = 16


def _cparams(*sem):
    return pltpu.CompilerParams(dimension_semantics=sem, vmem_limit_bytes=VMEM_LIMIT)


def _layer_norm(z, g, b):
    mu = jnp.mean(z, axis=-1, keepdims=True)
    d = z - mu
    var = jnp.mean(d * d, axis=-1, keepdims=True)
    return d * lax.rsqrt(var + LN_EPS) * g + b


def _dot(a, b):
    return jnp.dot(a, b, preferred_element_type=F32)


def _dot_nt(a, b):
    return lax.dot_general(a, b, (((1,), (1,)), ((), ())), preferred_element_type=F32)


def _rope(x, cos, s_up, s_dn):
    outs = []
    for h in range(N_HEADS):
        xh = x[:, h * V_DIM:(h + 1) * V_DIM]
        up = pltpu.roll(xh, V_DIM - ROT_DIM // 2, 1)
        dn = pltpu.roll(xh, ROT_DIM // 2, 1)
        outs.append(xh * cos + up * s_up + dn * s_dn)
    return outs


def _qkv_kernel(x_ref, w_ref, cos_ref, sup_ref, sdn_ref, *out_refs, prompt):
    y = _dot(x_ref[...].astype(BF16), w_ref[...])
    cos, s_up, s_dn = cos_ref[...], sup_ref[...], sdn_ref[...]
    q = _rope(y[:, :D_MODEL], cos, s_up, s_dn)
    k = _rope(y[:, D_MODEL:2 * D_MODEL], cos, s_up, s_dn)
    v = y[:, 2 * D_MODEL:]
    if prompt:
        kt_ref, v_ref, qm_ref, khm_ref, vt_ref = out_refs
        lane = lax.broadcasted_iota(jnp.int32, (1, V_DIM), 1)
        first = lane < HEAD_DIM
        for h in range(N_HEADS):
            cols = slice(h * V_DIM, (h + 1) * V_DIM)
            qh = q[h] * (QK_SCALE * LOG2E)
            qm_ref[0, h] = jnp.where(first, qh, 0.0).astype(BF16)
            qm_ref[1, h] = jnp.where(first, 0.0, qh).astype(BF16)
            khm_ref[h] = k[h].astype(BF16)
            kt_ref[0, cols, :] = k[h].T
            vt_ref[h, 0] = v[:, cols].T.astype(BF16)
    else:
        k_ref, v_ref, q_ref = out_refs
        for h in range(N_HEADS):
            cols = slice(h * V_DIM, (h + 1) * V_DIM)
            q_ref[:, cols] = q[h] * QK_SCALE
            k_ref[:, cols] = k[h]
    v_ref[...] = v


def _qkv_call(x, w_bf16, tables, *, tm, seq, prompt):
    t = x.shape[0]
    nblk = seq // tm
    row = pl.BlockSpec((tm, D_MODEL), lambda i: (i, 0))
    tab = pl.BlockSpec((tm, V_DIM), lambda i: (i % nblk, 0))
    rows_f32 = jax.ShapeDtypeStruct((t, D_MODEL), F32)
    if prompt:
        hm = pl.BlockSpec((N_HEADS, tm, V_DIM), lambda i: (0, i, 0))
        out_shape = [jax.ShapeDtypeStruct((t // seq, D_MODEL, seq), F32), rows_f32,
                     jax.ShapeDtypeStruct((2, N_HEADS, t, V_DIM), BF16),
                     jax.ShapeDtypeStruct((N_HEADS, t, V_DIM), BF16),
                     jax.ShapeDtypeStruct((N_HEADS, t // tm, V_DIM, tm), BF16)]
        out_specs = [pl.BlockSpec((1, D_MODEL, tm), lambda i: (i // nblk, 0, i % nblk)), row,
                     pl.BlockSpec((2, N_HEADS, tm, V_DIM), lambda i: (0, 0, i, 0)), hm,
                     pl.BlockSpec((N_HEADS, 1, V_DIM, tm), lambda i: (0, i, 0, 0))]
    else:
        out_shape = [rows_f32] * 3
        out_specs = [row] * 3
    return pl.pallas_call(
        functools.partial(_qkv_kernel, prompt=prompt),
        grid=(t // tm,),
        in_specs=[row, pl.BlockSpec((D_MODEL, 3 * D_MODEL), lambda i: (0, 0)), tab, tab, tab],
        out_specs=out_specs,
        out_shape=out_shape,
        compiler_params=_cparams("parallel"),
        name="qkv_rope",
    )(x, w_bf16, *tables)


def _rope_tables(pos):
    half = ROT_DIM // 2
    inv = ROPE_THETA ** (-jnp.arange(half, dtype=F32) * 2.0 / ROT_DIM)
    ang = pos.astype(F32)[:, None] * inv
    cos, sin = jnp.cos(ang), jnp.sin(ang)
    n = pos.shape[0]
    ones = jnp.ones((n, HEAD_DIM - ROT_DIM), F32)
    zeros = jnp.zeros((n, HEAD_DIM - ROT_DIM), F32)
    z8 = jnp.zeros((n, half), F32)
    c = jnp.concatenate([cos, cos, ones], axis=1)
    s_up = jnp.concatenate([-sin, z8, zeros], axis=1)
    s_dn = jnp.concatenate([z8, sin, zeros], axis=1)
    return tuple(jnp.tile(a, (1, 2)) for a in (c, s_up, s_dn))


def _diff_lambda(wl, lam_init):
    a = jnp.sum(wl[0:1] * wl[1:2], axis=-1, keepdims=True)
    b = jnp.sum(wl[2:3] * wl[3:4], axis=-1, keepdims=True)
    return jnp.exp(a) - jnp.exp(b) + lam_init


def _sub_norm(o, g, lam_init):
    o = o * lax.rsqrt(jnp.mean(o * o, axis=-1, keepdims=True) + LN_EPS)
    return o * g * (1.0 - lam_init)


def _softmax_step(s, m_ref, l_ref):
    m_prev = m_ref[...]
    m_new = jnp.maximum(m_prev, jnp.max(s, axis=-1, keepdims=True))
    alpha = jnp.exp(m_prev - m_new)
    p = jnp.exp(s - m_new[:, :1])
    l_ref[...] = alpha * l_ref[...] + jnp.sum(p, axis=-1, keepdims=True)
    m_ref[...] = m_new
    return alpha, p


def _attn_kernel(wl_ref, g_ref, q_ref, k_ref, vt_ref, o_ref, s_ref, p_ref, acc_ref, *, tq, kb, lam_init):
    qi = pl.program_id(2)
    q = q_ref[...].reshape(2 * tq, V_DIM)
    acc_ref[...] = jnp.zeros(acc_ref.shape, F32)
    nkb = tq // kb

    def scores(j):
        k = k_ref[0, pl.ds(pl.multiple_of(j * tq, tq), tq), :]
        st = _dot_nt(k, q)
        s_ref[...] = st
        return jnp.max(st, axis=0, keepdims=True)

    def values(j, alpha):
        vt = jnp.concatenate([vt_ref[0, j * nkb + b] for b in range(nkb)], axis=1)
        vt = jnp.concatenate([vt, jnp.ones((ONES_ROWS, tq), BF16)], axis=0)
        acc_ref[...] = alpha * acc_ref[...] + _dot(vt, p_ref[...])

    def body(j, carry):
        m, mt = carry
        m_new = jnp.maximum(m, mt)
        alpha = jnp.exp2(m - m_new)
        p_ref[...] = jnp.exp2(s_ref[...] - m_new).astype(BF16)
        mt_next = scores(j + 1)
        values(j, alpha)
        return m_new, mt_next

    init = (jnp.full((1, 2 * tq), -jnp.inf, F32), scores(0))
    m, _ = lax.fori_loop(0, qi, body, init)

    st = s_ref[...]
    key = lax.broadcasted_iota(jnp.int32, st.shape, 0)
    qry = lax.broadcasted_iota(jnp.int32, st.shape, 1)
    qry = jnp.where(qry >= tq, qry - tq, qry)
    st = jnp.where(key <= qry, st, -jnp.inf)
    m_new = jnp.maximum(m, jnp.max(st, axis=0, keepdims=True))
    p_ref[...] = jnp.exp2(st - m_new).astype(BF16)
    values(qi, jnp.exp2(m - m_new))

    acc = acc_ref[...]
    out_t = acc[:V_DIM] / acc[V_DIM:V_DIM + 1]
    lam = _diff_lambda(wl_ref[...], lam_init)
    o = (out_t[:, :tq] - lam * out_t[:, tq:]).T
    o_ref[...] = _sub_norm(o, g_ref[...], lam_init).astype(o_ref.dtype)


def _attn_call(qm, khm, vt, w_lambda, subln, *, batch, seq, lam_init):
    tq = ATT_TQ
    nq = seq // tq
    kb = vt.shape[-1]
    return pl.pallas_call(
        functools.partial(_attn_kernel, tq=tq, kb=kb, lam_init=lam_init),
        grid=(batch, N_HEADS, nq),
        in_specs=[pl.BlockSpec((4, HEAD_DIM), lambda b, h, i: (0, 0)),
                  pl.BlockSpec((1, V_DIM), lambda b, h, i: (0, 0)),
                  pl.BlockSpec((2, 1, tq, V_DIM), lambda b, h, i: (0, h, b * nq + i, 0)),
                  pl.BlockSpec((1, seq, V_DIM), lambda b, h, i: (h, b, 0)),
                  pl.BlockSpec((1, seq // kb, V_DIM, kb), lambda b, h, i: (h, b, 0, 0))],
        out_specs=pl.BlockSpec((tq, V_DIM), lambda b, h, i: (b * nq + i, h)),
        out_shape=jax.ShapeDtypeStruct((batch * seq, D_MODEL), BF16),
        scratch_shapes=[pltpu.VMEM((tq, 2 * tq), F32), pltpu.VMEM((tq, 2 * tq), BF16),
                        pltpu.VMEM((V_DIM + ONES_ROWS, 2 * tq), F32)],
        compiler_params=_cparams("parallel", "parallel", "arbitrary"),
        name="attn_prompt",
    )(w_lambda, subln, qm, khm, vt)


def _dec_attn_kernel(pt_ref, wl_ref, g_ref, q_ref, kn_ref, vn_ref, e_ref, *rest, n_pages, lam_init):
    k_refs = rest[:DEC_NP]
    v_refs = rest[DEC_NP:2 * DEC_NP]
    o_ref, qrow_ref, m_ref, l_ref, acc_ref = rest[2 * DEC_NP:]
    step_id = pl.program_id(1)
    n_rows = 2 * N_HEADS

    @pl.when(step_id == 0)
    def _():
        m_ref[...] = jnp.full(m_ref.shape, -jnp.inf, F32)
        l_ref[...] = jnp.zeros(l_ref.shape, F32)
        acc_ref[...] = jnp.zeros(acc_ref.shape, F32)
        row = lax.broadcasted_iota(jnp.int32, (n_rows, D_MODEL), 0)
        lane = lax.broadcasted_iota(jnp.int32, (n_rows, D_MODEL), 1)
        block = (row & (N_HEADS - 1)) * 2 + (row >> 3)
        qrow_ref[...] = jnp.where((lane >> 6) == block, q_ref[0], 0.0)

    qrows = qrow_ref[...]
    qb = qrows.astype(BF16)
    kcat = jnp.concatenate([k_refs[i][0, 0].astype(BF16) for i in range(DEC_NP)], axis=1)
    alpha, p = _softmax_step(_dot(qb, kcat), m_ref, l_ref)
    pb = p.astype(BF16)
    pstack = jnp.concatenate([pb[:, i * PAGE_SIZE:(i + 1) * PAGE_SIZE] for i in range(DEC_NP)], axis=0)
    pexp = _dot(pstack, e_ref[...])
    row = lax.broadcasted_iota(jnp.int32, pexp.shape, 0)
    lane = lax.broadcasted_iota(jnp.int32, pexp.shape, 1)
    own_head = (lane & (N_HEADS - 1)) == (row & (N_HEADS - 1))
    pexp = jnp.where(own_head, pexp, 0.0).astype(BF16)
    pcat = jnp.concatenate([pexp[i * n_rows:(i + 1) * n_rows] for i in range(DEC_NP)], axis=1)
    vcat = jnp.concatenate([v_refs[i][0, 0].astype(BF16) for i in range(DEC_NP)], axis=0)
    acc_ref[...] = alpha * acc_ref[...] + _dot(pcat, vcat)

    @pl.when(step_id == n_pages // DEC_NP - 1)
    def _():
        s = jnp.sum(qrows * kn_ref[0], axis=-1, keepdims=True)
        m_prev = m_ref[...]
        m_new = jnp.maximum(m_prev, s)
        alpha = jnp.exp(m_prev - m_new)
        p = jnp.exp(s - m_new[:, :1])
        l = alpha * l_ref[...] + p
        vn = vn_ref[0]
        acc = alpha * acc_ref[...] + p * jnp.concatenate([vn, vn], axis=0)
        out = acc / l
        lam = _diff_lambda(wl_ref[...], lam_init)
        o = out[:N_HEADS] - lam * out[N_HEADS:]
        o_ref[0] = _sub_norm(o, g_ref[...], lam_init).astype(o_ref.dtype)


def _dec_attn_call(page_table, q, k_new, v_new, cache_k, cache_v, w_lambda, subln, *, lam_init):
    nseq, n_pages = page_table.shape
    row = pl.BlockSpec((1, 1, D_MODEL), lambda b, p, pt: (b, 0, 0))
    heads = pl.BlockSpec((1, N_HEADS, V_DIM), lambda b, p, pt: (b, 0, 0))

    def page_spec(i):
        return pl.BlockSpec((1, 1, D_MODEL, PAGE_SIZE), lambda b, p, pt: (0, pt[b, p * DEC_NP + i], 0, 0))

    pages = [page_spec(i) for i in range(DEC_NP)]
    expand = (jnp.arange(PAGE_SIZE * N_HEADS)[None, :] // N_HEADS == jnp.arange(PAGE_SIZE)[:, None]).astype(BF16)
    grid_spec = pltpu.PrefetchScalarGridSpec(
        num_scalar_prefetch=1,
        grid=(nseq, n_pages // DEC_NP),
        in_specs=[pl.BlockSpec((4, HEAD_DIM), lambda b, p, pt: (0, 0)),
                  pl.BlockSpec((1, V_DIM), lambda b, p, pt: (0, 0)),
                  row, row, heads,
                  pl.BlockSpec((PAGE_SIZE, PAGE_SIZE * N_HEADS), lambda b, p, pt: (0, 0))] + pages + pages,
        out_specs=heads,
        scratch_shapes=[pltpu.VMEM((2 * N_HEADS, D_MODEL), F32),
                        pltpu.VMEM((2 * N_HEADS, LANES), F32),
                        pltpu.VMEM((2 * N_HEADS, LANES), F32),
                        pltpu.VMEM((2 * N_HEADS, V_DIM), F32)],
    )
    r3 = lambda a: a.reshape(nseq, 1, D_MODEL)
    out = pl.pallas_call(
        functools.partial(_dec_attn_kernel, n_pages=n_pages, lam_init=lam_init),
        grid_spec=grid_spec,
        out_shape=jax.ShapeDtypeStruct((nseq, N_HEADS, V_DIM), BF16),
        compiler_params=_cparams("parallel", "arbitrary"),
        name="attn_decode",
    )(page_table, w_lambda, subln, r3(q), r3(k_new), v_new.reshape(nseq, N_HEADS, V_DIM), expand,
      *([cache_k] * DEC_NP), *([cache_v] * DEC_NP))
    return out.reshape(nseq, D_MODEL)


def _proj_ln_kernel(o_ref, w_ref, x_ref, g_ref, b_ref, out_ref):
    z = ALPHA * x_ref[...] + _dot(o_ref[...], w_ref[...])
    out_ref[...] = _layer_norm(z, g_ref[...], b_ref[...])


def _proj_ln_call(o, w_bf16, x, g, b, *, tm):
    t = x.shape[0]
    row = pl.BlockSpec((tm, D_MODEL), lambda i: (i, 0))
    vec = pl.BlockSpec((1, D_MODEL), lambda i: (0, 0))
    return pl.pallas_call(
        _proj_ln_kernel,
        grid=(t // tm,),
        in_specs=[row, pl.BlockSpec((D_MODEL, D_MODEL), lambda i: (0, 0)), row, vec, vec],
        out_specs=row,
        out_shape=jax.ShapeDtypeStruct((t, D_MODEL), F32),
        compiler_params=_cparams("parallel"),
        name="proj_ln",
    )(o, w_bf16, x, g, b)


def _route(scores, sel):
    def pair_max(vals):
        best = None
        for i in range(len(vals)):
            for j in range(i + 1, len(vals)):
                s = vals[i] + vals[j]
                best = s if best is None else jnp.maximum(best, s)
        return best

    grp = [sel[g * EXPERTS_PER_GROUP:(g + 1) * EXPERTS_PER_GROUP] for g in range(N_EXPERT_GROUPS)]
    grp_score = [pair_max(v) for v in grp]
    best, g_idx = grp_score[0], jnp.zeros_like(grp_score[0], dtype=jnp.int32)
    for g in range(1, N_EXPERT_GROUPS):
        upd = grp_score[g] > best
        g_idx = jnp.where(upd, g, g_idx)
        best = jnp.where(upd, grp_score[g], best)

    def pick(rows, j):
        out = rows[j]
        for g in range(1, N_EXPERT_GROUPS):
            out = jnp.where(g_idx == g, rows[g * EXPERTS_PER_GROUP + j], out)
        return out

    in_sel = [pick(sel, j) for j in range(EXPERTS_PER_GROUP)]
    in_sc = [pick(scores, j) for j in range(EXPERTS_PER_GROUP)]

    def arg_first_max(vals, excluded):
        bv, bi = None, None
        for j, v in enumerate(vals):
            v = v if excluded is None else jnp.where(excluded == j, -jnp.inf, v)
            if bv is None:
                bv, bi = v, jnp.zeros_like(v, dtype=jnp.int32)
            else:
                upd = v > bv
                bi = jnp.where(upd, j, bi)
                bv = jnp.where(upd, v, bv)
        return bi

    l1 = arg_first_max(in_sel, None)
    l2 = arg_first_max(in_sel, l1)

    def take(vals, idx):
        out = vals[0]
        for j in range(1, len(vals)):
            out = jnp.where(idx == j, vals[j], out)
        return out

    w1, w2 = take(in_sc, l1), take(in_sc, l2)
    tot = w1 + w2
    w1, w2 = w1 / tot, w2 / tot
    gates = []
    for e in range(N_EXPERTS):
        g, j = divmod(e, EXPERTS_PER_GROUP)
        local = jnp.where(l1 == j, w1, jnp.where(l2 == j, w2, 0.0))
        gates.append(jnp.where(g_idx == g, local, 0.0))
    return gates


def _moe_ln_kernel(x_ref, wr_ref, rb_ref, win_ref, wout_ref, g_ref, b_ref, out_ref, hcat_ref):
    x = x_ref[...]
    tm = x.shape[0]
    xb = x.astype(BF16)
    x_lo = (x - xb.astype(F32)).astype(BF16)
    wr = wr_ref[...]
    hi_hilo = _dot(xb, wr)
    logits = hi_hilo[:, :LANES] + hi_hilo[:, LANES:] + _dot(x_lo, wr[:, :LANES])
    scores_t = jax.nn.sigmoid(logits.T[:N_EXPERTS])
    sel_t = scores_t + rb_ref[...]
    rows = lambda a: [a[e:e + 1] for e in range(N_EXPERTS)]
    gates = _route(rows(scores_t), rows(sel_t))
    gates_t = jnp.concatenate(gates + [jnp.zeros((LANES - N_EXPERTS, tm), F32)], axis=0)
    gate_cols = gates_t.T
    for e in range(N_EXPERTS):
        h = _dot(xb, win_ref[e])
        a = jax.nn.silu(h[:, :D_EXPERT]) * h[:, D_EXPERT:] * gate_cols[:, e:e + 1]
        hcat_ref[:, e * D_EXPERT:(e + 1) * D_EXPERT] = a.astype(BF16)
    z = ALPHA * x + _dot(hcat_ref[...], wout_ref[...])
    out_ref[...] = _layer_norm(z, g_ref[...], b_ref[...])


def _moe_ln_call(x, wr_pad, rb_col, win_bf16, wout_bf16, g, b, *, tm):
    t = x.shape[0]
    row = pl.BlockSpec((tm, D_MODEL), lambda i: (i, 0))
    vec = pl.BlockSpec((1, D_MODEL), lambda i: (0, 0))
    once = pl.Buffered(1)
    return pl.pallas_call(
        _moe_ln_kernel,
        grid=(t // tm,),
        in_specs=[row,
                  pl.BlockSpec((D_MODEL, 2 * LANES), lambda i: (0, 0)),
                  pl.BlockSpec((N_EXPERTS, 1), lambda i: (0, 0)),
                  pl.BlockSpec((N_EXPERTS, D_MODEL, 2 * D_EXPERT), lambda i: (0, 0, 0), pipeline_mode=once),
                  pl.BlockSpec((N_EXPERTS * D_EXPERT, D_MODEL), lambda i: (0, 0), pipeline_mode=once),
                  vec, vec],
        out_specs=row,
        out_shape=jax.ShapeDtypeStruct((t, D_MODEL), F32),
        scratch_shapes=[pltpu.VMEM((tm, N_EXPERTS * D_EXPERT), BF16)],
        compiler_params=_cparams("parallel"),
        name="moe_ln",
    )(x, wr_pad, rb_col, win_bf16, wout_bf16, g, b)


def _ssm_param_kernel(ar_ref, ai_ref, ls_ref, bre_ref, bim_ref,
                      abr_ref, abi_ref, anr_ref, ani_ref, wre_ref, wim_ref, *, n_sq):
    ar, ai = ar_ref[...], ai_ref[...]
    dt = jnp.exp(ls_ref[...])
    mag = jnp.exp(ar * dt)
    abr, abi = mag * jnp.cos(ai * dt), mag * jnp.sin(ai * dt)
    den = ar * ar + ai * ai
    cr = ((abr - 1.0) * ar + abi * ai) / den
    ci = (abi * ar - (abr - 1.0) * ai) / den
    bre, bim = bre_ref[...], bim_ref[...]
    wre_ref[...] = cr * bre - ci * bim
    wim_ref[...] = cr * bim + ci * bre
    abr_ref[...] = abr
    abi_ref[...] = abi
    pr, pi = abr, abi
    for _ in range(n_sq):
        pr, pi = pr * pr - pi * pi, 2.0 * pr * pi
    anr_ref[...] = pr
    ani_ref[...] = pi


def _ssm_param_call(a_re, a_im, log_step, b_re, b_im):
    col = lambda a: a.reshape(N_STATES, 1)
    ls = jnp.broadcast_to(log_step[:, None], (N_SSM_GROUPS, SSM_STATE))
    wide = lambda a: a.reshape(N_STATES, SSM_GROUP)
    cshape = jax.ShapeDtypeStruct((N_STATES, 1), F32)
    wshape = jax.ShapeDtypeStruct((N_STATES, SSM_GROUP), F32)
    return pl.pallas_call(
        functools.partial(_ssm_param_kernel, n_sq=int(math.log2(SSM_NA))),
        out_shape=[cshape] * 4 + [wshape] * 2,
        compiler_params=pltpu.CompilerParams(vmem_limit_bytes=VMEM_LIMIT),
        name="ssm_params",
    )(col(a_re), col(a_im), col(ls), wide(b_re), wide(b_im))


def _block_diag_in(w):
    gl = N_SSM_GROUPS // SSM_BLK
    w = w.reshape(SSM_BLK, gl, SSM_STATE, SSM_GROUP)
    eye = jnp.eye(gl, dtype=w.dtype)
    full = w.transpose(0, 1, 3, 2)[:, :, :, None, :] * eye[None, :, None, :, None]
    return full.reshape(SSM_BLK, gl * SSM_GROUP, gl * SSM_STATE)


def _block_diag_out(c):
    gl = N_SSM_GROUPS // SSM_BLK
    c = c.reshape(SSM_BLK, gl, SSM_GROUP, SSM_STATE)
    eye = jnp.eye(gl, dtype=c.dtype)
    full = c.transpose(0, 1, 3, 2)[:, :, :, None, :] * eye[None, :, None, :, None]
    return full.reshape(SSM_BLK, gl * SSM_STATE, gl * SSM_GROUP)


def _glu_ln(u, y, d_ref, wglu_ref, g_ref, b_ref):
    y = y + d_ref[...] * u
    gl = jax.nn.gelu(y).astype(BF16)
    vg = _dot(gl, wglu_ref[...])
    mix = vg[:, :D_MODEL] * jax.nn.sigmoid(vg[:, D_MODEL:])
    return _layer_norm(ALPHA * u + mix, g_ref[...], b_ref[...])


def _ssm_scan_kernel(u_ref, abr_ref, abi_ref, anr_ref, ani_ref, wb_ref, wc_ref, d_ref, wglu_ref, g_ref, b_ref,
                     out_ref, hre_ref, him_ref, u_scr, sre_scr, sim_scr, cre_scr, cim_scr):
    chunk = pl.program_id(1)
    na, tc = SSM_NA, SSM_TC

    @pl.when(chunk == 0)
    def _():
        cre_scr[...] = jnp.zeros(cre_scr.shape, F32)
        cim_scr[...] = jnp.zeros(cim_scr.shape, F32)

    ncol = D_MODEL // LANES
    for a in range(na):
        for c in range(ncol):
            u_scr[a * SUBLANES:(a + 1) * SUBLANES, c * LANES:(c + 1) * LANES] = (
                u_ref[pl.ds(a * ncol + c, SUBLANES, stride=ncol * na), :])
    u = u_scr[...]
    ub = u.astype(BF16)
    half = N_STATES // SSM_BLK
    for j in range(SSM_BLK):
        bu = _dot(ub[:, j * MXU_DIM:(j + 1) * MXU_DIM], wb_ref[j])
        for w in range(half // SSM_LW):
            lg = j * (half // SSM_LW) + w
            sre_scr[lg] = bu[:, w * SSM_LW:(w + 1) * SSM_LW]
            sim_scr[lg] = bu[:, half + w * SSM_LW:half + (w + 1) * SSM_LW]

    def lane_group(lg, carry):
        bshape = (SUBLANES, SSM_LW)
        ar = jnp.broadcast_to(abr_ref[lg], bshape)
        ai = jnp.broadcast_to(abi_ref[lg], bshape)

        def rows(a):
            return pl.ds(pl.multiple_of(a * SUBLANES, SUBLANES), SUBLANES)

        def pass1(a, h):
            hr, hi = h
            return (ar * hr - ai * hi + sre_scr[lg, rows(a), :], ar * hi + ai * hr + sim_scr[lg, rows(a), :])

        zero = jnp.zeros(bshape, F32)
        er, ei = lax.fori_loop(0, na, pass1, (zero, zero), unroll=4)

        anr, ani = anr_ref[lg], ani_ref[lg]
        sr, si = cre_scr[lg], cim_scr[lg]
        starts_r, starts_i = [], []
        for r in range(SUBLANES):
            starts_r.append(sr)
            starts_i.append(si)
            sr, si = (anr * sr - ani * si + er[r:r + 1], anr * si + ani * sr + ei[r:r + 1])
        cre_scr[lg] = sr
        cim_scr[lg] = si

        def pass2(a, h):
            hr, hi = h
            nr = ar * hr - ai * hi + sre_scr[lg, rows(a), :]
            ni = ar * hi + ai * hr + sim_scr[lg, rows(a), :]
            sre_scr[lg, rows(a), :] = nr
            sim_scr[lg, rows(a), :] = ni
            return nr, ni

        start = (jnp.concatenate(starts_r, axis=0), jnp.concatenate(starts_i, axis=0))
        lax.fori_loop(0, na, pass2, start, unroll=4)
        return carry

    lax.fori_loop(0, SSM_NLG, lane_group, 0)

    for lg in range(SSM_NLG):
        hre_ref[0, :, lg * SSM_LW:(lg + 1) * SSM_LW] = cre_scr[lg]
        him_ref[0, :, lg * SSM_LW:(lg + 1) * SSM_LW] = cim_scr[lg]

    ys = []
    per = half // SSM_LW
    for j in range(SSM_BLK):
        parts = [sre_scr[j * per + w] for w in range(per)] + [sim_scr[j * per + w] for w in range(per)]
        hcat = jnp.concatenate(parts, axis=1).astype(BF16)
        ys.append(_dot(hcat, wc_ref[j]))
    y = jnp.concatenate(ys, axis=1)
    res = _glu_ln(u, y, d_ref, wglu_ref, g_ref, b_ref)
    for a in range(na):
        for c in range(ncol):
            out_ref[pl.ds(a * ncol + c, SUBLANES, stride=ncol * na), :] = (
                res[a * SUBLANES:(a + 1) * SUBLANES, c * LANES:(c + 1) * LANES])


def _ssm_scan_call(u, abr, abi, anr, ani, wb, wc, d_skip, wglu, g, b, *, batch, seq):
    nchunk = seq // SSM_TC
    ncol = D_MODEL // LANES
    row = pl.BlockSpec((SSM_TC * ncol, LANES), lambda bb, c: (bb * nchunk + c, 0))
    vec = pl.BlockSpec((1, D_MODEL), lambda bb, c: (0, 0))
    lanes3 = pl.BlockSpec((SSM_NLG, 1, SSM_LW), lambda bb, c: (0, 0, 0))
    state = pl.BlockSpec((1, 1, N_STATES), lambda bb, c: (bb, 0, 0))
    sshape = jax.ShapeDtypeStruct((batch, 1, N_STATES), F32)
    g3 = lambda a: a.reshape(SSM_NLG, 1, SSM_LW)
    out, sre, sim = pl.pallas_call(
        _ssm_scan_kernel,
        grid=(batch, nchunk),
        in_specs=[row, lanes3, lanes3, lanes3, lanes3,
                  pl.BlockSpec(wb.shape, lambda bb, c: (0, 0, 0)),
                  pl.BlockSpec(wc.shape, lambda bb, c: (0, 0, 0)),
                  vec, pl.BlockSpec((D_MODEL, 2 * D_MODEL), lambda bb, c: (0, 0)), vec, vec],
        out_specs=[row, state, state],
        out_shape=[jax.ShapeDtypeStruct((batch * seq * ncol, LANES), F32), sshape, sshape],
        scratch_shapes=[pltpu.VMEM((SSM_TC, D_MODEL), F32),
                        pltpu.VMEM((SSM_NLG, SSM_TC, SSM_LW), F32),
                        pltpu.VMEM((SSM_NLG, SSM_TC, SSM_LW), F32),
                        pltpu.VMEM((SSM_NLG, 1, SSM_LW), F32),
                        pltpu.VMEM((SSM_NLG, 1, SSM_LW), F32)],
        compiler_params=_cparams("parallel", "arbitrary"),
        name="ssm_scan",
    )(u.reshape(batch * seq * ncol, LANES), g3(abr), g3(abi), g3(anr), g3(ani), wb, wc, d_skip, wglu, g, b)
    return out.reshape(batch * seq, D_MODEL), sre, sim


def _ssm_step_kernel(u_ref, h0r_ref, h0i_ref, abr_ref, abi_ref, wb_ref, wc_ref, d_ref, wglu_ref, g_ref, b_ref,
                     out_ref, hre_ref, him_ref):
    u = u_ref[...]
    half = N_STATES // SSM_BLK
    abr, abi = abr_ref[...], abi_ref[...]
    h0r, h0i = h0r_ref[...], h0i_ref[...]
    ys = []
    for j in range(SSM_BLK):
        bu = jnp.dot(u[:, j * MXU_DIM:(j + 1) * MXU_DIM], wb_ref[j], preferred_element_type=F32,
                     precision=lax.Precision.HIGHEST)
        cols = slice(j * half, (j + 1) * half)
        hr = abr[:, cols] * h0r[:, cols] - abi[:, cols] * h0i[:, cols] + bu[:, :half]
        hi = abr[:, cols] * h0i[:, cols] + abi[:, cols] * h0r[:, cols] + bu[:, half:]
        hre_ref[:, cols] = hr
        him_ref[:, cols] = hi
        hcat = jnp.concatenate([hr, hi], axis=1).astype(BF16)
        ys.append(_dot(hcat, wc_ref[j]))
    y = jnp.concatenate(ys, axis=1)
    out_ref[...] = _glu_ln(u, y, d_ref, wglu_ref, g_ref, b_ref)


def _ssm_step_call(u, h0r, h0i, abr, abi, wb_f32, wc, d_skip, wglu, g, b):
    n = u.shape[0]
    sshape = jax.ShapeDtypeStruct((n, N_STATES), F32)
    return pl.pallas_call(
        _ssm_step_kernel,
        out_shape=[jax.ShapeDtypeStruct((n, D_MODEL), F32), sshape, sshape],
        compiler_params=pltpu.CompilerParams(vmem_limit_bytes=VMEM_LIMIT),
        name="ssm_step",
    )(u, h0r, h0i, abr, abi, wb_f32, wc, d_skip, wglu, g, b)


def kernel(x_prompt, x_sample, cache_k, cache_v, state_ssm_re, state_ssm_im, page_table, w_qkv, w_lambda,
           attn_subln, w_attn_out, ssm_a_re, ssm_a_im, ssm_log_step, ssm_b_re, ssm_b_im, ssm_c_re, ssm_c_im,
           ssm_d, w_glu, w_router, router_bias, w_expert_in, w_expert_out, ln_gain, ln_bias):
    bp, lp, _ = x_prompt.shape
    bs, ls, _ = x_sample.shape
    assert ls == 1 and lp % SSM_TC == 0 and lp % ATT_TQ == 0
    n_pages = page_table.shape[1]
    past_len = n_pages * PAGE_SIZE
    n_pool = cache_k.shape[1]
    tp = bp * lp
    xp = x_prompt.reshape(tp, D_MODEL)
    xs = x_sample.reshape(bs, D_MODEL)

    vec = lambda a: a.reshape(1, D_MODEL)
    wr_hi = w_router.astype(BF16)
    wr_lo = (w_router - wr_hi.astype(F32)).astype(BF16)
    lane_pad = lambda a: jnp.pad(a, ((0, 0), (0, LANES - N_EXPERTS)))
    wr_pad = jnp.concatenate([lane_pad(wr_hi), lane_pad(wr_lo)], axis=1)
    rb_col = router_bias.reshape(N_EXPERTS, 1)

    def moe(x, layer, tm):
        win = w_expert_in[layer].astype(BF16)
        wout = w_expert_out[layer].reshape(N_EXPERTS * D_EXPERT, D_MODEL).astype(BF16)
        n = x.shape[0]
        pad = -n % tm
        xpad = jnp.pad(x, ((0, pad), (0, 0))) if pad else x
        out = _moe_ln_call(xpad, wr_pad, rb_col, win, wout, vec(ln_gain[layer, 1]), vec(ln_bias[layer, 1]), tm=tm)
        return out[:n] if pad else out

    lam_init = 0.8 - 0.6 * math.exp(-0.3 * 0)
    wqkv = w_qkv[0].astype(BF16)
    wo = w_attn_out[0].astype(BF16)
    subln = attn_subln[0].reshape(1, V_DIM)
    tab_p = _rope_tables(jnp.arange(lp, dtype=jnp.int32))
    tab_s = _rope_tables(jnp.broadcast_to(past_len + jnp.arange(ls, dtype=jnp.int32), (bs,)))
    kt_p, v_p, qm, khm, vt = _qkv_call(xp, wqkv, tab_p, tm=QKV_TM, seq=lp, prompt=True)
    k_s, v_s, q_s = _qkv_call(xs, wqkv, tab_s, tm=bs, seq=bs, prompt=False)
    o_p = _attn_call(qm, khm, vt, w_lambda[0], subln, batch=bp, seq=lp, lam_init=lam_init)
    ck = cache_k[0].transpose(0, 2, 3, 4, 1).reshape(1, n_pool, D_MODEL, PAGE_SIZE)
    cv = cache_v[0].reshape(1, n_pool, PAGE_SIZE * N_HEADS, V_DIM)
    o_s = _dec_attn_call(page_table, q_s, k_s, v_s, ck, cv, w_lambda[0], subln, lam_init=lam_init)
    g0, b0 = vec(ln_gain[0, 0]), vec(ln_bias[0, 0])
    hp = _proj_ln_call(o_p, wo, xp, g0, b0, tm=ROW_TM)
    hs = _proj_ln_call(o_s, wo, xs, g0, b0, tm=bs)
    hp = moe(hp, 0, MOE_TM)
    hs = moe(hs, 0, LANES)

    abr, abi, anr, ani, wre, wim = _ssm_param_call(ssm_a_re[0], ssm_a_im[0], ssm_log_step[0],
                                                   ssm_b_re[0], ssm_b_im[0])
    lane_row = lambda a: a.reshape(1, N_STATES)
    wb = jnp.concatenate([_block_diag_in(wre), _block_diag_in(wim)], axis=-1)
    wc = jnp.concatenate([_block_diag_out(ssm_c_re[0]), -_block_diag_out(ssm_c_im[0])], axis=1)
    wc = wc.astype(BF16)
    wglu = w_glu[0].astype(BF16)
    d_skip = vec(ssm_d[0])
    g1, b1 = vec(ln_gain[1, 0]), vec(ln_bias[1, 0])
    hp, sre_p, sim_p = _ssm_scan_call(hp, abr, abi, anr, ani, wb.astype(BF16), wc, d_skip, wglu, g1, b1,
                                      batch=bp, seq=lp)
    hs, sre_s, sim_s = _ssm_step_call(hs, state_ssm_re[0].reshape(bs, N_STATES),
                                      state_ssm_im[0].reshape(bs, N_STATES),
                                      lane_row(abr), lane_row(abi), wb, wc, d_skip, wglu, g1, b1)
    hp = moe(hp, 1, MOE_TM)
    hs = moe(hs, 1, LANES)

    st = lambda a, n: a.reshape(1, n, N_SSM_GROUPS, SSM_STATE)
    return (hp.reshape(bp, lp, D_MODEL), hs.reshape(bs, ls, D_MODEL),
            kt_p.reshape(1, bp, N_HEADS, 2, HEAD_DIM, lp).transpose(0, 1, 5, 2, 3, 4),
            v_p.reshape(1, bp, lp, N_HEADS, V_DIM),
            st(sre_p, bp), st(sim_p, bp),
            k_s.reshape(1, bs, ls, N_HEADS, 2, HEAD_DIM), v_s.reshape(1, bs, ls, N_HEADS, V_DIM),
            st(sre_s, bs), st(sim_s, bs))
```

```python
import functools
import math

import jax
import jax.numpy as jnp
from jax import lax
from jax.experimental import pallas as pl
from jax.experimental.pallas import tpu as pltpu

F32 = jnp.float32
BF16 = jnp.bfloat16

D_MODEL = 1024
DEPTH = 2
PAGE_SIZE = 128
N_HEADS = 8
HEAD_DIM = 64
V_DIM = 2 * HEAD_DIM
ROT_DIM = HEAD_DIM // 4
ROPE_THETA = 500000.0
SSM_GROUP = 16
N_SSM_GROUPS = D_MODEL // SSM_GROUP
SSM_STATE = 64
N_STATES = N_SSM_GROUPS * SSM_STATE
N_EXPERTS = 16
N_EXPERT_GROUPS = 4
EXPERTS_PER_GROUP = N_EXPERTS // N_EXPERT_GROUPS
D_EXPERT = 256
ALPHA = (2 * DEPTH) ** 0.25
LN_EPS = 1e-5
QK_SCALE = HEAD_DIM ** -0.5
LOG2E = math.log2(math.e)

LANES = 128
SUBLANES = 8
MXU_DIM = 256
VMEM_LIMIT = 56 * 1024 * 1024

QKV_TM = 256
ATT_TQ = 512
ROW_TM = 512
MOE_TM = 256
SSM_TC = 256
SSM_NA = SSM_TC // SUBLANES
SSM_LW = 512
SSM_NLG = N_STATES // SSM_LW
SSM_BLK = 4
DEC_NP = 16
DEC_SUB = 1
ONES_ROWS = 16


def _cparams(*sem):
    return pltpu.CompilerParams(dimension_semantics=sem, vmem_limit_bytes=VMEM_LIMIT)


def _layer_norm(z, g, b):
    mu = jnp.mean(z, axis=-1, keepdims=True)
    d = z - mu
    var = jnp.mean(d * d, axis=-1, keepdims=True)
    return d * lax.rsqrt(var + LN_EPS) * g + b


def _dot(a, b):
    return jnp.dot(a, b, preferred_element_type=F32)


def _dot_nt(a, b):
    return lax.dot_general(a, b, (((1,), (1,)), ((), ())), preferred_element_type=F32)


def _rope(x, cos, s_up, s_dn):
    outs = []
    for h in range(N_HEADS):
        xh = x[:, h * V_DIM:(h + 1) * V_DIM]
        up = pltpu.roll(xh, V_DIM - ROT_DIM // 2, 1)
        dn = pltpu.roll(xh, ROT_DIM // 2, 1)
        outs.append(xh * cos + up * s_up + dn * s_dn)
    return outs


def _qkv_kernel(x_ref, w_ref, cos_ref, sup_ref, sdn_ref, *out_refs, prompt):
    y = _dot(x_ref[...].astype(BF16), w_ref[...])
    cos, s_up, s_dn = cos_ref[...], sup_ref[...], sdn_ref[...]
    q = _rope(y[:, :D_MODEL], cos, s_up, s_dn)
    k = _rope(y[:, D_MODEL:2 * D_MODEL], cos, s_up, s_dn)
    v = y[:, 2 * D_MODEL:]
    if prompt:
        kt_ref, v_ref, qm_ref, khm_ref, vt_ref = out_refs
        lane = lax.broadcasted_iota(jnp.int32, (1, V_DIM), 1)
        first = lane < HEAD_DIM
        for h in range(N_HEADS):
            cols = slice(h * V_DIM, (h + 1) * V_DIM)
            qh = q[h] * (QK_SCALE * LOG2E)
            qm_ref[0, h] = jnp.where(first, qh, 0.0).astype(BF16)
            qm_ref[1, h] = jnp.where(first, 0.0, qh).astype(BF16)
            khm_ref[h] = k[h].astype(BF16)
            kt_ref[0, cols, :] = k[h].T
            vt_ref[h, 0] = v[:, cols].T.astype(BF16)
    else:
        k_ref, v_ref, q_ref = out_refs
        for h in range(N_HEADS):
            cols = slice(h * V_DIM, (h + 1) * V_DIM)
            q_ref[:, cols] = q[h] * QK_SCALE
            k_ref[:, cols] = k[h]
    v_ref[...] = v


def _qkv_call(x, w_bf16, tables, *, tm, seq, prompt):
    t = x.shape[0]
    nblk = seq // tm
    row = pl.BlockSpec((tm, D_MODEL), lambda i: (i, 0))
    tab = pl.BlockSpec((tm, V_DIM), lambda i: (i % nblk, 0))
    rows_f32 = jax.ShapeDtypeStruct((t, D_MODEL), F32)
    if prompt:
        hm = pl.BlockSpec((N_HEADS, tm, V_DIM), lambda i: (0, i, 0))
        out_shape = [jax.ShapeDtypeStruct((t // seq, D_MODEL, seq), F32), rows_f32,
                     jax.ShapeDtypeStruct((2, N_HEADS, t, V_DIM), BF16),
                     jax.ShapeDtypeStruct((N_HEADS, t, V_DIM), BF16),
                     jax.ShapeDtypeStruct((N_HEADS, t // tm, V_DIM, tm), BF16)]
        out_specs = [pl.BlockSpec((1, D_MODEL, tm), lambda i: (i // nblk, 0, i % nblk)), row,
                     pl.BlockSpec((2, N_HEADS, tm, V_DIM), lambda i: (0, 0, i, 0)), hm,
                     pl.BlockSpec((N_HEADS, 1, V_DIM, tm), lambda i: (0, i, 0, 0))]
    else:
        out_shape = [rows_f32] * 3
        out_specs = [row] * 3
    return pl.pallas_call(
        functools.partial(_qkv_kernel, prompt=prompt),
        grid=(t // tm,),
        in_specs=[row, pl.BlockSpec((D_MODEL, 3 * D_MODEL), lambda i: (0, 0)), tab, tab, tab],
        out_specs=out_specs,
        out_shape=out_shape,
        compiler_params=_cparams("parallel"),
        name="qkv_rope",
    )(x, w_bf16, *tables)


def _rope_tables(pos):
    half = ROT_DIM // 2
    inv = ROPE_THETA ** (-jnp.arange(half, dtype=F32) * 2.0 / ROT_DIM)
    ang = pos.astype(F32)[:, None] * inv
    cos, sin = jnp.cos(ang), jnp.sin(ang)
    n = pos.shape[0]
    ones = jnp.ones((n, HEAD_DIM - ROT_DIM), F32)
    zeros = jnp.zeros((n, HEAD_DIM - ROT_DIM), F32)
    z8 = jnp.zeros((n, half), F32)
    c = jnp.concatenate([cos, cos, ones], axis=1)
    s_up = jnp.concatenate([-sin, z8, zeros], axis=1)
    s_dn = jnp.concatenate([z8, sin, zeros], axis=1)
    return tuple(jnp.tile(a, (1, 2)) for a in (c, s_up, s_dn))


def _diff_lambda(wl, lam_init):
    a = jnp.sum(wl[0:1] * wl[1:2], axis=-1, keepdims=True)
    b = jnp.sum(wl[2:3] * wl[3:4], axis=-1, keepdims=True)
    return jnp.exp(a) - jnp.exp(b) + lam_init


def _sub_norm(o, g, lam_init):
    o = o * lax.rsqrt(jnp.mean(o * o, axis=-1, keepdims=True) + LN_EPS)
    return o * g * (1.0 - lam_init)


def _softmax_step(s, m_ref, l_ref):
    m_prev = m_ref[...]
    m_new = jnp.maximum(m_prev, jnp.max(s, axis=-1, keepdims=True))
    alpha = jnp.exp(m_prev - m_new)
    p = jnp.exp(s - m_new[:, :1])
    l_ref[...] = alpha * l_ref[...] + jnp.sum(p, axis=-1, keepdims=True)
    m_ref[...] = m_new
    return alpha, p


def _decode_init(q_ref, qrow_ref, m_ref, l_ref, acc_ref):
    n_rows = 2 * N_HEADS
    m_ref[...] = jnp.full(m_ref.shape, -jnp.inf, F32)
    l_ref[...] = jnp.zeros(l_ref.shape, F32)
    acc_ref[...] = jnp.zeros(acc_ref.shape, F32)
    row = lax.broadcasted_iota(jnp.int32, (n_rows, D_MODEL), 0)
    lane = lax.broadcasted_iota(jnp.int32, (n_rows, D_MODEL), 1)
    block = (row & (N_HEADS - 1)) * 2 + (row >> 3)
    qrow_ref[...] = jnp.where((lane >> 6) == block, q_ref[0], 0.0)


def _decode_pages(k_refs, v_refs, e_ref, qrow_ref, m_ref, l_ref, acc_ref):
    n_rows = 2 * N_HEADS
    npg = len(k_refs)
    qb = qrow_ref[...].astype(BF16)
    kcat = jnp.concatenate([r[0, 0].astype(BF16) for r in k_refs], axis=1)
    alpha, p = _softmax_step(_dot(qb, kcat), m_ref, l_ref)
    pb = p.astype(BF16)
    pstack = jnp.concatenate([pb[:, i * PAGE_SIZE:(i + 1) * PAGE_SIZE] for i in range(npg)], axis=0)
    pexp = _dot(pstack, e_ref[...])
    row = lax.broadcasted_iota(jnp.int32, pexp.shape, 0)
    lane = lax.broadcasted_iota(jnp.int32, pexp.shape, 1)
    own_head = (lane & (N_HEADS - 1)) == (row & (N_HEADS - 1))
    pexp = jnp.where(own_head, pexp, 0.0).astype(BF16)
    pcat = jnp.concatenate([pexp[i * n_rows:(i + 1) * n_rows] for i in range(npg)], axis=1)
    vcat = jnp.concatenate([r[0, 0].astype(BF16) for r in v_refs], axis=0)
    acc_ref[...] = alpha * acc_ref[...] + _dot(pcat, vcat)


def _decode_finish(wl_ref, g_ref, kn_ref, vn_ref, o_ref, qrow_ref, m_ref, l_ref, acc_ref, lam_init):
    s = jnp.sum(qrow_ref[...] * kn_ref[0], axis=-1, keepdims=True)
    m_prev = m_ref[...]
    m_new = jnp.maximum(m_prev, s)
    alpha = jnp.exp(m_prev - m_new)
    p = jnp.exp(s - m_new[:, :1])
    l = alpha * l_ref[...] + p
    vn = vn_ref[0]
    acc = alpha * acc_ref[...] + p * jnp.concatenate([vn, vn], axis=0)
    out = acc / l
    lam = _diff_lambda(wl_ref[...], lam_init)
    o = out[:N_HEADS] - lam * out[N_HEADS:]
    o_ref[0] = _sub_norm(o, g_ref[...], lam_init).astype(o_ref.dtype)


def _attn_kernel(pt_ref, wl_ref, g_ref, q_ref, k_ref, vt_ref, qs_ref, kn_ref, vn_ref, e_ref, *rest,
                 tq, kb, lam_init, steps_per_seq):
    npg = DEC_SUB * DEC_NP
    kp_refs, vp_refs = rest[:npg], rest[npg:2 * npg]
    o_ref, os_ref, s_ref, p_ref, acc_ref, qrow_ref, dm_ref, dl_ref, dacc_ref = rest[2 * npg:]
    qi = pl.program_id(2)
    step = (pl.program_id(0) * pl.num_programs(1) + pl.program_id(1)) * pl.num_programs(2) + qi
    phase = step % steps_per_seq
    dec_state = (qrow_ref, dm_ref, dl_ref, dacc_ref)

    @pl.when(phase == 0)
    def _():
        _decode_init(qs_ref, *dec_state)

    for u in range(DEC_SUB):
        _decode_pages(kp_refs[u * DEC_NP:(u + 1) * DEC_NP], vp_refs[u * DEC_NP:(u + 1) * DEC_NP], e_ref, *dec_state)

    q = q_ref[...].reshape(2 * tq, V_DIM)
    acc_ref[...] = jnp.zeros(acc_ref.shape, F32)
    nkb = tq // kb

    def scores(j):
        k = k_ref[0, pl.ds(pl.multiple_of(j * tq, tq), tq), :]
        st = _dot_nt(k, q)
        s_ref[...] = st
        return jnp.max(st, axis=0, keepdims=True)

    def values(j, alpha):
        vt = jnp.concatenate([vt_ref[0, j * nkb + b] for b in range(nkb)], axis=1)
        vt = jnp.concatenate([vt, jnp.ones((ONES_ROWS, tq), BF16)], axis=0)
        acc_ref[...] = alpha * acc_ref[...] + _dot(vt, p_ref[...])

    def body(j, carry):
        m, mt = carry
        m_new = jnp.maximum(m, mt)
        alpha = jnp.exp2(m - m_new)
        p_ref[...] = jnp.exp2(s_ref[...] - m_new).astype(BF16)
        mt_next = scores(j + 1)
        values(j, alpha)
        return m_new, mt_next

    init = (jnp.full((1, 2 * tq), -jnp.inf, F32), scores(0))
    m, _ = lax.fori_loop(0, qi, body, init)

    st = s_ref[...]
    key = lax.broadcasted_iota(jnp.int32, st.shape, 0)
    qry = lax.broadcasted_iota(jnp.int32, st.shape, 1)
    qry = jnp.where(qry >= tq, qry - tq, qry)
    st = jnp.where(key <= qry, st, -jnp.inf)
    m_new = jnp.maximum(m, jnp.max(st, axis=0, keepdims=True))
    p_ref[...] = jnp.exp2(st - m_new).astype(BF16)
    values(qi, jnp.exp2(m - m_new))

    acc = acc_ref[...]
    out_t = acc[:V_DIM] / acc[V_DIM:V_DIM + 1]
    lam = _diff_lambda(wl_ref[...], lam_init)
    o = (out_t[:, :tq] - lam * out_t[:, tq:]).T
    o_ref[...] = _sub_norm(o, g_ref[...], lam_init).astype(o_ref.dtype)

    @pl.when(phase == steps_per_seq - 1)
    def _():
        _decode_finish(wl_ref, g_ref, kn_ref, vn_ref, os_ref, *dec_state, lam_init)


def _attn_call(qm, khm, vt, w_lambda, subln, page_table, q_s, k_s, v_s, cache_k, cache_v, *, batch, seq, lam_init):
    tq = ATT_TQ
    nq = seq // tq
    kb = vt.shape[-1]
    nseq, n_pages = page_table.shape
    npg = DEC_SUB * DEC_NP
    steps_per_seq = n_pages // npg
    assert batch * N_HEADS * nq == nseq * steps_per_seq
    step_of = lambda b, h, i: (b * N_HEADS + h) * nq + i
    seq_of = lambda b, h, i: step_of(b, h, i) // steps_per_seq
    row = pl.BlockSpec((1, 1, D_MODEL), lambda b, h, i, pt: (seq_of(b, h, i), 0, 0))
    heads = pl.BlockSpec((1, N_HEADS, V_DIM), lambda b, h, i, pt: (seq_of(b, h, i), 0, 0))

    def page_spec(j):
        def index(b, h, i, pt):
            s = step_of(b, h, i)
            return (0, pt[s // steps_per_seq, (s % steps_per_seq) * npg + j], 0, 0)
        return pl.BlockSpec((1, 1, D_MODEL, PAGE_SIZE), index)

    pages = [page_spec(j) for j in range(npg)]
    expand = (jnp.arange(PAGE_SIZE * N_HEADS)[None, :] // N_HEADS == jnp.arange(PAGE_SIZE)[:, None]).astype(BF16)
    dec_rows = 2 * N_HEADS
    grid_spec = pltpu.PrefetchScalarGridSpec(
        num_scalar_prefetch=1,
        grid=(batch, N_HEADS, nq),
        in_specs=[pl.BlockSpec((4, HEAD_DIM), lambda b, h, i, pt: (0, 0)),
                  pl.BlockSpec((1, V_DIM), lambda b, h, i, pt: (0, 0)),
                  pl.BlockSpec((2, 1, tq, V_DIM), lambda b, h, i, pt: (0, h, b * nq + i, 0)),
                  pl.BlockSpec((1, seq, V_DIM), lambda b, h, i, pt: (h, b, 0)),
                  pl.BlockSpec((1, seq // kb, V_DIM, kb), lambda b, h, i, pt: (h, b, 0, 0)),
                  row, row, heads,
                  pl.BlockSpec((PAGE_SIZE, PAGE_SIZE * N_HEADS), lambda b, h, i, pt: (0, 0))] + pages + pages,
        out_specs=[pl.BlockSpec((tq, V_DIM), lambda b, h, i, pt: (b * nq + i, h)), heads],
        scratch_shapes=[pltpu.VMEM((tq, 2 * tq), F32), pltpu.VMEM((tq, 2 * tq), BF16),
                        pltpu.VMEM((V_DIM + ONES_ROWS, 2 * tq), F32),
                        pltpu.VMEM((dec_rows, D_MODEL), F32), pltpu.VMEM((dec_rows, LANES), F32),
                        pltpu.VMEM((dec_rows, LANES), F32), pltpu.VMEM((dec_rows, V_DIM), F32)],
    )
    r3 = lambda a: a.reshape(nseq, 1, D_MODEL)
    o_p, o_s = pl.pallas_call(
        functools.partial(_attn_kernel, tq=tq, kb=kb, lam_init=lam_init, steps_per_seq=steps_per_seq),
        grid_spec=grid_spec,
        out_shape=[jax.ShapeDtypeStruct((batch * seq, D_MODEL), BF16),
                   jax.ShapeDtypeStruct((nseq, N_HEADS, V_DIM), BF16)],
        compiler_params=_cparams("arbitrary", "arbitrary", "arbitrary"),
        name="attn",
    )(page_table, w_lambda, subln, qm, khm, vt, r3(q_s), r3(k_s), v_s.reshape(nseq, N_HEADS, V_DIM), expand,
      *([cache_k] * npg), *([cache_v] * npg))
    return o_p, o_s.reshape(nseq, D_MODEL)


def _proj_ln_kernel(o_ref, w_ref, x_ref, g_ref, b_ref, out_ref):
    z = ALPHA * x_ref[...] + _dot(o_ref[...], w_ref[...])
    out_ref[...] = _layer_norm(z, g_ref[...], b_ref[...])


def _proj_ln_call(o, w_bf16, x, g, b, *, tm):
    t = x.shape[0]
    row = pl.BlockSpec((tm, D_MODEL), lambda i: (i, 0))
    vec = pl.BlockSpec((1, D_MODEL), lambda i: (0, 0))
    return pl.pallas_call(
        _proj_ln_kernel,
        grid=(t // tm,),
        in_specs=[row, pl.BlockSpec((D_MODEL, D_MODEL), lambda i: (0, 0)), row, vec, vec],
        out_specs=row,
        out_shape=jax.ShapeDtypeStruct((t, D_MODEL), F32),
        compiler_params=_cparams("parallel"),
        name="proj_ln",
    )(o, w_bf16, x, g, b)


def _route(scores, sel):
    def pair_max(vals):
        best = None
        for i in range(len(vals)):
            for j in range(i + 1, len(vals)):
                s = vals[i] + vals[j]
                best = s if best is None else jnp.maximum(best, s)
        return best

    grp = [sel[g * EXPERTS_PER_GROUP:(g + 1) * EXPERTS_PER_GROUP] for g in range(N_EXPERT_GROUPS)]
    grp_score = [pair_max(v) for v in grp]
    best, g_idx = grp_score[0], jnp.zeros_like(grp_score[0], dtype=jnp.int32)
    for g in range(1, N_EXPERT_GROUPS):
        upd = grp_score[g] > best
        g_idx = jnp.where(upd, g, g_idx)
        best = jnp.where(upd, grp_score[g], best)

    def pick(rows, j):
        out = rows[j]
        for g in range(1, N_EXPERT_GROUPS):
            out = jnp.where(g_idx == g, rows[g * EXPERTS_PER_GROUP + j], out)
        return out

    in_sel = [pick(sel, j) for j in range(EXPERTS_PER_GROUP)]
    in_sc = [pick(scores, j) for j in range(EXPERTS_PER_GROUP)]

    def arg_first_max(vals, excluded):
        bv, bi = None, None
        for j, v in enumerate(vals):
            v = v if excluded is None else jnp.where(excluded == j, -jnp.inf, v)
            if bv is None:
                bv, bi = v, jnp.zeros_like(v, dtype=jnp.int32)
            else:
                upd = v > bv
                bi = jnp.where(upd, j, bi)
                bv = jnp.where(upd, v, bv)
        return bi

    l1 = arg_first_max(in_sel, None)
    l2 = arg_first_max(in_sel, l1)

    def take(vals, idx):
        out = vals[0]
        for j in range(1, len(vals)):
            out = jnp.where(idx == j, vals[j], out)
        return out

    w1, w2 = take(in_sc, l1), take(in_sc, l2)
    tot = w1 + w2
    w1, w2 = w1 / tot, w2 / tot
    gates = []
    for e in range(N_EXPERTS):
        g, j = divmod(e, EXPERTS_PER_GROUP)
        local = jnp.where(l1 == j, w1, jnp.where(l2 == j, w2, 0.0))
        gates.append(jnp.where(g_idx == g, local, 0.0))
    return gates


def _moe_ln_kernel(x_ref, wr_ref, rb_ref, win_ref, wout_ref, g_ref, b_ref, out_ref, hcat_ref):
    x = x_ref[...]
    tm = x.shape[0]
    xb = x.astype(BF16)
    x_lo = (x - xb.astype(F32)).astype(BF16)
    wr = wr_ref[...]
    hi_hilo = _dot(xb, wr)
    logits = hi_hilo[:, :LANES] + hi_hilo[:, LANES:] + _dot(x_lo, wr[:, :LANES])
    scores_t = jax.nn.sigmoid(logits.T[:N_EXPERTS])
    sel_t = scores_t + rb_ref[...]
    rows = lambda a: [a[e:e + 1] for e in range(N_EXPERTS)]
    gates = _route(rows(scores_t), rows(sel_t))
    gates_t = jnp.concatenate(gates + [jnp.zeros((LANES - N_EXPERTS, tm), F32)], axis=0)
    gate_cols = gates_t.T
    for e in range(N_EXPERTS):
        h = _dot(xb, win_ref[e])
        a = jax.nn.silu(h[:, :D_EXPERT]) * h[:, D_EXPERT:] * gate_cols[:, e:e + 1]
        hcat_ref[:, e * D_EXPERT:(e + 1) * D_EXPERT] = a.astype(BF16)
    z = ALPHA * x + _dot(hcat_ref[...], wout_ref[...])
    out_ref[...] = _layer_norm(z, g_ref[...], b_ref[...])


def _moe_ln_call(x, wr_pad, rb_col, win_bf16, wout_bf16, g, b, *, tm):
    t = x.shape[0]
    row = pl.BlockSpec((tm, D_MODEL), lambda i: (i, 0))
    vec = pl.BlockSpec((1, D_MODEL), lambda i: (0, 0))
    once = pl.Buffered(1)
    return pl.pallas_call(
        _moe_ln_kernel,
        grid=(t // tm,),
        in_specs=[row,
                  pl.BlockSpec((D_MODEL, 2 * LANES), lambda i: (0, 0)),
                  pl.BlockSpec((N_EXPERTS, 1), lambda i: (0, 0)),
                  pl.BlockSpec((N_EXPERTS, D_MODEL, 2 * D_EXPERT), lambda i: (0, 0, 0), pipeline_mode=once),
                  pl.BlockSpec((N_EXPERTS * D_EXPERT, D_MODEL), lambda i: (0, 0), pipeline_mode=once),
                  vec, vec],
        out_specs=row,
        out_shape=jax.ShapeDtypeStruct((t, D_MODEL), F32),
        scratch_shapes=[pltpu.VMEM((tm, N_EXPERTS * D_EXPERT), BF16)],
        compiler_params=_cparams("parallel"),
        name="moe_ln",
    )(x, wr_pad, rb_col, win_bf16, wout_bf16, g, b)


def _ssm_param_kernel(ar_ref, ai_ref, ls_ref, bre_ref, bim_ref,
                      abr_ref, abi_ref, anr_ref, ani_ref, wre_ref, wim_ref, *, n_sq):
    ar, ai = ar_ref[...], ai_ref[...]
    dt = jnp.exp(ls_ref[...])
    mag = jnp.exp(ar * dt)
    abr, abi = mag * jnp.cos(ai * dt), mag * jnp.sin(ai * dt)
    den = ar * ar + ai * ai
    cr = ((abr - 1.0) * ar + abi * ai) / den
    ci = (abi * ar - (abr - 1.0) * ai) / den
    bre, bim = bre_ref[...], bim_ref[...]
    wre_ref[...] = cr * bre - ci * bim
    wim_ref[...] = cr * bim + ci * bre
    abr_ref[...] = abr
    abi_ref[...] = abi
    pr, pi = abr, abi
    for _ in range(n_sq):
        pr, pi = pr * pr - pi * pi, 2.0 * pr * pi
    anr_ref[...] = pr
    ani_ref[...] = pi


def _ssm_param_call(a_re, a_im, log_step, b_re, b_im):
    col = lambda a: a.reshape(N_STATES, 1)
    ls = jnp.broadcast_to(log_step[:, None], (N_SSM_GROUPS, SSM_STATE))
    wide = lambda a: a.reshape(N_STATES, SSM_GROUP)
    cshape = jax.ShapeDtypeStruct((N_STATES, 1), F32)
    wshape = jax.ShapeDtypeStruct((N_STATES, SSM_GROUP), F32)
    return pl.pallas_call(
        functools.partial(_ssm_param_kernel, n_sq=int(math.log2(SSM_NA))),
        out_shape=[cshape] * 4 + [wshape] * 2,
        compiler_params=pltpu.CompilerParams(vmem_limit_bytes=VMEM_LIMIT),
        name="ssm_params",
    )(col(a_re), col(a_im), col(ls), wide(b_re), wide(b_im))


def _block_diag_in(w):
    gl = N_SSM_GROUPS // SSM_BLK
    w = w.reshape(SSM_BLK, gl, SSM_STATE, SSM_GROUP)
    eye = jnp.eye(gl, dtype=w.dtype)
    full = w.transpose(0, 1, 3, 2)[:, :, :, None, :] * eye[None, :, None, :, None]
    return full.reshape(SSM_BLK, gl * SSM_GROUP, gl * SSM_STATE)


def _block_diag_out(c):
    gl = N_SSM_GROUPS // SSM_BLK
    c = c.reshape(SSM_BLK, gl, SSM_GROUP, SSM_STATE)
    eye = jnp.eye(gl, dtype=c.dtype)
    full = c.transpose(0, 1, 3, 2)[:, :, :, None, :] * eye[None, :, None, :, None]
    return full.reshape(SSM_BLK, gl * SSM_STATE, gl * SSM_GROUP)


def _glu_ln(u, y, d_ref, wglu_ref, g_ref, b_ref):
    y = y + d_ref[...] * u
    gl = jax.nn.gelu(y).astype(BF16)
    vg = _dot(gl, wglu_ref[...])
    mix = vg[:, :D_MODEL] * jax.nn.sigmoid(vg[:, D_MODEL:])
    return _layer_norm(ALPHA * u + mix, g_ref[...], b_ref[...])


def _ssm_scan_kernel(u_ref, perm_ref, unperm_ref, abr_ref, abi_ref, anr_ref, ani_ref, wb_ref, wc_ref, d_ref,
                     wglu_ref, g_ref, b_ref, out_ref, hre_ref, him_ref, sre_scr, sim_scr, cre_scr, cim_scr):
    chunk = pl.program_id(1)
    na = SSM_NA

    @pl.when(chunk == 0)
    def _():
        cre_scr[...] = jnp.zeros(cre_scr.shape, F32)
        cim_scr[...] = jnp.zeros(cim_scr.shape, F32)

    u = u_ref[...]
    ub = _dot(perm_ref[...], u.astype(BF16)).astype(BF16)
    half = N_STATES // SSM_BLK
    for j in range(SSM_BLK):
        bu = _dot(ub[:, j * MXU_DIM:(j + 1) * MXU_DIM], wb_ref[j])
        for w in range(half // SSM_LW):
            lg = j * (half // SSM_LW) + w
            sre_scr[lg] = bu[:, w * SSM_LW:(w + 1) * SSM_LW]
            sim_scr[lg] = bu[:, half + w * SSM_LW:half + (w + 1) * SSM_LW]

    def lane_group(lg, carry):
        bshape = (SUBLANES, SSM_LW)
        ar = jnp.broadcast_to(abr_ref[lg], bshape)
        ai = jnp.broadcast_to(abi_ref[lg], bshape)

        def rows(a):
            return pl.ds(pl.multiple_of(a * SUBLANES, SUBLANES), SUBLANES)

        def pass1(a, h):
            hr, hi = h
            return (ar * hr - ai * hi + sre_scr[lg, rows(a), :], ar * hi + ai * hr + sim_scr[lg, rows(a), :])

        zero = jnp.zeros(bshape, F32)
        er, ei = lax.fori_loop(0, na, pass1, (zero, zero), unroll=4)

        anr, ani = anr_ref[lg], ani_ref[lg]
        sr, si = cre_scr[lg], cim_scr[lg]
        starts_r, starts_i = [], []
        for r in range(SUBLANES):
            starts_r.append(sr)
            starts_i.append(si)
            sr, si = (anr * sr - ani * si + er[r:r + 1], anr * si + ani * sr + ei[r:r + 1])
        cre_scr[lg] = sr
        cim_scr[lg] = si

        def pass2(a, h):
            hr, hi = h
            nr = ar * hr - ai * hi + sre_scr[lg, rows(a), :]
            ni = ar * hi + ai * hr + sim_scr[lg, rows(a), :]
            sre_scr[lg, rows(a), :] = nr
            sim_scr[lg, rows(a), :] = ni
            return nr, ni

        start = (jnp.concatenate(starts_r, axis=0), jnp.concatenate(starts_i, axis=0))
        lax.fori_loop(0, na, pass2, start, unroll=4)
        return carry

    lax.fori_loop(0, SSM_NLG, lane_group, 0)

    for lg in range(SSM_NLG):
        hre_ref[0, :, lg * SSM_LW:(lg + 1) * SSM_LW] = cre_scr[lg]
        him_ref[0, :, lg * SSM_LW:(lg + 1) * SSM_LW] = cim_scr[lg]

    ys = []
    per = half // SSM_LW
    for j in range(SSM_BLK):
        parts = [sre_scr[j * per + w] for w in range(per)] + [sim_scr[j * per + w] for w in range(per)]
        hcat = jnp.concatenate(parts, axis=1).astype(BF16)
        ys.append(_dot(hcat, wc_ref[j]))
    y = jnp.concatenate(ys, axis=1)
    y_hi = y.astype(BF16)
    y_lo = (y - y_hi.astype(F32)).astype(BF16)
    y = _dot(unperm_ref[...], y_hi) + _dot(unperm_ref[...], y_lo)
    out_ref[...] = _glu_ln(u, y, d_ref, wglu_ref, g_ref, b_ref)


def _ssm_scan_call(u, abr, abi, anr, ani, wb, wc, d_skip, wglu, g, b, *, batch, seq):
    nchunk = seq // SSM_TC
    row = pl.BlockSpec((SSM_TC, D_MODEL), lambda bb, c: (bb * nchunk + c, 0))
    vec = pl.BlockSpec((1, D_MODEL), lambda bb, c: (0, 0))
    lanes3 = pl.BlockSpec((SSM_NLG, 1, SSM_LW), lambda bb, c: (0, 0, 0))
    state = pl.BlockSpec((1, 1, N_STATES), lambda bb, c: (bb, 0, 0))
    square = pl.BlockSpec((SSM_TC, SSM_TC), lambda bb, c: (0, 0))
    sshape = jax.ShapeDtypeStruct((batch, 1, N_STATES), F32)
    g3 = lambda a: a.reshape(SSM_NLG, 1, SSM_LW)
    rows = jnp.arange(SSM_TC)
    perm = (rows[None, :] == ((rows % SUBLANES) * SSM_NA + rows // SUBLANES)[:, None]).astype(BF16)
    out, sre, sim = pl.pallas_call(
        _ssm_scan_kernel,
        grid=(batch, nchunk),
        in_specs=[row, square, square, lanes3, lanes3, lanes3, lanes3,
                  pl.BlockSpec(wb.shape, lambda bb, c: (0, 0, 0)),
                  pl.BlockSpec(wc.shape, lambda bb, c: (0, 0, 0)),
                  vec, pl.BlockSpec((D_MODEL, 2 * D_MODEL), lambda bb, c: (0, 0)), vec, vec],
        out_specs=[row, state, state],
        out_shape=[jax.ShapeDtypeStruct((batch * seq, D_MODEL), F32), sshape, sshape],
        scratch_shapes=[pltpu.VMEM((SSM_NLG, SSM_TC, SSM_LW), F32),
                        pltpu.VMEM((SSM_NLG, SSM_TC, SSM_LW), F32),
                        pltpu.VMEM((SSM_NLG, 1, SSM_LW), F32),
                        pltpu.VMEM((SSM_NLG, 1, SSM_LW), F32)],
        compiler_params=_cparams("parallel", "arbitrary"),
        name="ssm_scan",
    )(u, perm, perm.T, g3(abr), g3(abi), g3(anr), g3(ani), wb, wc, d_skip, wglu, g, b)
    return out, sre, sim


def _ssm_step_kernel(u_ref, h0r_ref, h0i_ref, abr_ref, abi_ref, wb_ref, wc_ref, d_ref, wglu_ref, g_ref, b_ref,
                     out_ref, hre_ref, him_ref):
    u = u_ref[...]
    half = N_STATES // SSM_BLK
    abr, abi = abr_ref[...], abi_ref[...]
    h0r, h0i = h0r_ref[...], h0i_ref[...]
    ys = []
    for j in range(SSM_BLK):
        bu = jnp.dot(u[:, j * MXU_DIM:(j + 1) * MXU_DIM], wb_ref[j], preferred_element_type=F32,
                     precision=lax.Precision.HIGHEST)
        cols = slice(j * half, (j + 1) * half)
        hr = abr[:, cols] * h0r[:, cols] - abi[:, cols] * h0i[:, cols] + bu[:, :half]
        hi = abr[:, cols] * h0i[:, cols] + abi[:, cols] * h0r[:, cols] + bu[:, half:]
        hre_ref[:, cols] = hr
        him_ref[:, cols] = hi
        hcat = jnp.concatenate([hr, hi], axis=1).astype(BF16)
        ys.append(_dot(hcat, wc_ref[j]))
    y = jnp.concatenate(ys, axis=1)
    out_ref[...] = _glu_ln(u, y, d_ref, wglu_ref, g_ref, b_ref)


def _ssm_step_call(u, h0r, h0i, abr, abi, wb_f32, wc, d_skip, wglu, g, b):
    n = u.shape[0]
    sshape = jax.ShapeDtypeStruct((n, N_STATES), F32)
    return pl.pallas_call(
        _ssm_step_kernel,
        out_shape=[jax.ShapeDtypeStruct((n, D_MODEL), F32), sshape, sshape],
        compiler_params=pltpu.CompilerParams(vmem_limit_bytes=VMEM_LIMIT),
        name="ssm_step",
    )(u, h0r, h0i, abr, abi, wb_f32, wc, d_skip, wglu, g, b)


def kernel(x_prompt, x_sample, cache_k, cache_v, state_ssm_re, state_ssm_im, page_table, w_qkv, w_lambda,
           attn_subln, w_attn_out, ssm_a_re, ssm_a_im, ssm_log_step, ssm_b_re, ssm_b_im, ssm_c_re, ssm_c_im,
           ssm_d, w_glu, w_router, router_bias, w_expert_in, w_expert_out, ln_gain, ln_bias):
    bp, lp, _ = x_prompt.shape
    bs, ls, _ = x_sample.shape
    assert ls == 1 and lp % SSM_TC == 0 and lp % ATT_TQ == 0
    n_pages = page_table.shape[1]
    past_len = n_pages * PAGE_SIZE
    n_pool = cache_k.shape[1]
    tp = bp * lp
    xp = x_prompt.reshape(tp, D_MODEL)
    xs = x_sample.reshape(bs, D_MODEL)

    vec = lambda a: a.reshape(1, D_MODEL)
    wr_hi = w_router.astype(BF16)
    wr_lo = (w_router - wr_hi.astype(F32)).astype(BF16)
    lane_pad = lambda a: jnp.pad(a, ((0, 0), (0, LANES - N_EXPERTS)))
    wr_pad = jnp.concatenate([lane_pad(wr_hi), lane_pad(wr_lo)], axis=1)
    rb_col = router_bias.reshape(N_EXPERTS, 1)

    def moe(x, layer, tm):
        win = w_expert_in[layer].astype(BF16)
        wout = w_expert_out[layer].reshape(N_EXPERTS * D_EXPERT, D_MODEL).astype(BF16)
        n = x.shape[0]
        pad = -n % tm
        xpad = jnp.pad(x, ((0, pad), (0, 0))) if pad else x
        out = _moe_ln_call(xpad, wr_pad, rb_col, win, wout, vec(ln_gain[layer, 1]), vec(ln_bias[layer, 1]), tm=tm)
        return out[:n] if pad else out

    lam_init = 0.8 - 0.6 * math.exp(-0.3 * 0)
    wqkv = w_qkv[0].astype(BF16)
    wo = w_attn_out[0].astype(BF16)
    subln = attn_subln[0].reshape(1, V_DIM)
    tab_p = _rope_tables(jnp.arange(lp, dtype=jnp.int32))
    tab_s = _rope_tables(jnp.broadcast_to(past_len + jnp.arange(ls, dtype=jnp.int32), (bs,)))
    kt_p, v_p, qm, khm, vt = _qkv_call(xp, wqkv, tab_p, tm=QKV_TM, seq=lp, prompt=True)
    k_s, v_s, q_s = _qkv_call(xs, wqkv, tab_s, tm=bs, seq=bs, prompt=False)
    ck = cache_k[0].transpose(0, 2, 3, 4, 1).reshape(1, n_pool, D_MODEL, PAGE_SIZE)
    cv = cache_v[0].reshape(1, n_pool, PAGE_SIZE * N_HEADS, V_DIM)
    o_p, o_s = _attn_call(qm, khm, vt, w_lambda[0], subln, page_table, q_s, k_s, v_s, ck, cv,
                          batch=bp, seq=lp, lam_init=lam_init)
    g0, b0 = vec(ln_gain[0, 0]), vec(ln_bias[0, 0])
    hp = _proj_ln_call(o_p, wo, xp, g0, b0, tm=ROW_TM)
    hs = _proj_ln_call(o_s, wo, xs, g0, b0, tm=bs)
    hp = moe(hp, 0, MOE_TM)
    hs = moe(hs, 0, LANES)

    abr, abi, anr, ani, wre, wim = _ssm_param_call(ssm_a_re[0], ssm_a_im[0], ssm_log_step[0],
                                                   ssm_b_re[0], ssm_b_im[0])
    lane_row = lambda a: a.reshape(1, N_STATES)
    wb = jnp.concatenate([_block_diag_in(wre), _block_diag_in(wim)], axis=-1)
    wc = jnp.concatenate([_block_diag_out(ssm_c_re[0]), -_block_diag_out(ssm_c_im[0])], axis=1)
    wc = wc.astype(BF16)
    wglu = w_glu[0].astype(BF16)
    d_skip = vec(ssm_d[0])
    g1, b1 = vec(ln_gain[1, 0]), vec(ln_bias[1, 0])
    hp, sre_p, sim_p = _ssm_scan_call(hp, abr, abi, anr, ani, wb.astype(BF16), wc, d_skip, wglu, g1, b1,
                                      batch=bp, seq=lp)
    hs, sre_s, sim_s = _ssm_step_call(hs, state_ssm_re[0].reshape(bs, N_STATES),
                                      state_ssm_im[0].reshape(bs, N_STATES),
                                      lane_row(abr), lane_row(abi), wb, wc, d_skip, wglu, g1, b1)
    hp = moe(hp, 1, MOE_TM)
    hs = moe(hs, 1, LANES)

    st = lambda a, n: a.reshape(1, n, N_SSM_GROUPS, SSM_STATE)
    return (hp.reshape(bp, lp, D_MODEL), hs.reshape(bs, ls, D_MODEL),
            kt_p.reshape(1, bp, N_HEADS, 2, HEAD_DIM, lp).transpose(0, 1, 5, 2, 3, 4),
            v_p.reshape(1, bp, lp, N_HEADS, V_DIM),
            st(sre_p, bp), st(sim_p, bp),
            k_s.reshape(1, bs, ls, N_HEADS, 2, HEAD_DIM), v_s.reshape(1, bs, ls, N_HEADS, V_DIM),
            st(sre_s, bs), st(sim_s, bs))
```

```python
import functools
import math

import jax
import jax.numpy as jnp
from jax import lax
from jax.experimental import pallas as pl
from jax.experimental.pallas import tpu as pltpu

F32 = jnp.float32
BF16 = jnp.bfloat16

D_MODEL = 1024
DEPTH = 2
PAGE_SIZE = 128
N_HEADS = 8
HEAD_DIM = 64
V_DIM = 2 * HEAD_DIM
ROT_DIM = HEAD_DIM // 4
ROPE_THETA = 500000.0
SSM_GROUP = 16
N_SSM_GROUPS = D_MODEL // SSM_GROUP
SSM_STATE = 64
N_STATES = N_SSM_GROUPS * SSM_STATE
N_EXPERTS = 16
N_EXPERT_GROUPS = 4
EXPERTS_PER_GROUP = N_EXPERTS // N_EXPERT_GROUPS
D_EXPERT = 256
ALPHA = (2 * DEPTH) ** 0.25
LN_EPS = 1e-5
QK_SCALE = HEAD_DIM ** -0.5
LOG2E = math.log2(math.e)

LANES = 128
SUBLANES = 8
MXU_DIM = 256
VMEM_LIMIT = 56 * 1024 * 1024

QKV_TM = 256
ATT_TQ = 512
ROW_TM = 512
MOE_TM = 256
SSM_TC = 256
SSM_NA = SSM_TC // SUBLANES
SSM_LW = 512
SSM_NLG = N_STATES // SSM_LW
SSM_BLK = 4
DEC_NP = 16
DEC_SUB = 1
ONES_ROWS = 16


def _cparams(*sem):
    return pltpu.CompilerParams(dimension_semantics=sem, vmem_limit_bytes=VMEM_LIMIT)


def _layer_norm(z, g, b):
    mu = jnp.mean(z, axis=-1, keepdims=True)
    d = z - mu
    var = jnp.mean(d * d, axis=-1, keepdims=True)
    return d * lax.rsqrt(var + LN_EPS) * g + b


def _dot(a, b):
    return jnp.dot(a, b, preferred_element_type=F32)


def _dot_nt(a, b):
    return lax.dot_general(a, b, (((1,), (1,)), ((), ())), preferred_element_type=F32)


def _rope(x, cos, s_up, s_dn):
    outs = []
    for h in range(N_HEADS):
        xh = x[:, h * V_DIM:(h + 1) * V_DIM]
        up = pltpu.roll(xh, V_DIM - ROT_DIM // 2, 1)
        dn = pltpu.roll(xh, ROT_DIM // 2, 1)
        outs.append(xh * cos + up * s_up + dn * s_dn)
    return outs


def _qkv_kernel(x_ref, w_ref, cos_ref, sup_ref, sdn_ref, *out_refs, prompt):
    y = _dot(x_ref[...].astype(BF16), w_ref[...])
    cos, s_up, s_dn = cos_ref[...], sup_ref[...], sdn_ref[...]
    q = _rope(y[:, :D_MODEL], cos, s_up, s_dn)
    k = _rope(y[:, D_MODEL:2 * D_MODEL], cos, s_up, s_dn)
    v = y[:, 2 * D_MODEL:]
    if prompt:
        kt_ref, v_ref, qm_ref, khm_ref, vt_ref = out_refs
        lane = lax.broadcasted_iota(jnp.int32, (1, V_DIM), 1)
        first = lane < HEAD_DIM
        for h in range(N_HEADS):
            cols = slice(h * V_DIM, (h + 1) * V_DIM)
            qh = q[h] * (QK_SCALE * LOG2E)
            qm_ref[0, h] = jnp.where(first, qh, 0.0).astype(BF16)
            qm_ref[1, h] = jnp.where(first, 0.0, qh).astype(BF16)
            khm_ref[h] = k[h].astype(BF16)
            kt_ref[0, cols, :] = k[h].T
            vt_ref[h, 0] = v[:, cols].T.astype(BF16)
    else:
        k_ref, v_ref, q_ref = out_refs
        for h in range(N_HEADS):
            cols = slice(h * V_DIM, (h + 1) * V_DIM)
            q_ref[:, cols] = q[h] * QK_SCALE
            k_ref[:, cols] = k[h]
    v_ref[...] = v


def _qkv_call(x, w_bf16, tables, *, tm, seq, prompt):
    t = x.shape[0]
    nblk = seq // tm
    row = pl.BlockSpec((tm, D_MODEL), lambda i: (i, 0))
    tab = pl.BlockSpec((tm, V_DIM), lambda i: (i % nblk, 0))
    rows_f32 = jax.ShapeDtypeStruct((t, D_MODEL), F32)
    if prompt:
        hm = pl.BlockSpec((N_HEADS, tm, V_DIM), lambda i: (0, i, 0))
        out_shape = [jax.ShapeDtypeStruct((t // seq, D_MODEL, seq), F32), rows_f32,
                     jax.ShapeDtypeStruct((2, N_HEADS, t, V_DIM), BF16),
                     jax.ShapeDtypeStruct((N_HEADS, t, V_DIM), BF16),
                     jax.ShapeDtypeStruct((N_HEADS, t // tm, V_DIM, tm), BF16)]
        out_specs = [pl.BlockSpec((1, D_MODEL, tm), lambda i: (i // nblk, 0, i % nblk)), row,
                     pl.BlockSpec((2, N_HEADS, tm, V_DIM), lambda i: (0, 0, i, 0)), hm,
                     pl.BlockSpec((N_HEADS, 1, V_DIM, tm), lambda i: (0, i, 0, 0))]
    else:
        out_shape = [rows_f32] * 3
        out_specs = [row] * 3
    return pl.pallas_call(
        functools.partial(_qkv_kernel, prompt=prompt),
        grid=(t // tm,),
        in_specs=[row, pl.BlockSpec((D_MODEL, 3 * D_MODEL), lambda i: (0, 0)), tab, tab, tab],
        out_specs=out_specs,
        out_shape=out_shape,
        compiler_params=_cparams("parallel"),
        name="qkv_rope",
    )(x, w_bf16, *tables)


def _rope_tables(pos):
    half = ROT_DIM // 2
    inv = ROPE_THETA ** (-jnp.arange(half, dtype=F32) * 2.0 / ROT_DIM)
    ang = pos.astype(F32)[:, None] * inv
    cos, sin = jnp.cos(ang), jnp.sin(ang)
    n = pos.shape[0]
    ones = jnp.ones((n, HEAD_DIM - ROT_DIM), F32)
    zeros = jnp.zeros((n, HEAD_DIM - ROT_DIM), F32)
    z8 = jnp.zeros((n, half), F32)
    c = jnp.concatenate([cos, cos, ones], axis=1)
    s_up = jnp.concatenate([-sin, z8, zeros], axis=1)
    s_dn = jnp.concatenate([z8, sin, zeros], axis=1)
    return tuple(jnp.tile(a, (1, 2)) for a in (c, s_up, s_dn))


def _diff_lambda(wl, lam_init):
    a = jnp.sum(wl[0:1] * wl[1:2], axis=-1, keepdims=True)
    b = jnp.sum(wl[2:3] * wl[3:4], axis=-1, keepdims=True)
    return jnp.exp(a) - jnp.exp(b) + lam_init


def _sub_norm(o, g, lam_init):
    o = o * lax.rsqrt(jnp.mean(o * o, axis=-1, keepdims=True) + LN_EPS)
    return o * g * (1.0 - lam_init)


def _softmax_step(s, m_ref, l_ref):
    m_prev = m_ref[...]
    m_new = jnp.maximum(m_prev, jnp.max(s, axis=-1, keepdims=True))
    alpha = jnp.exp(m_prev - m_new)
    p = jnp.exp(s - m_new[:, :1])
    l_ref[...] = alpha * l_ref[...] + jnp.sum(p, axis=-1, keepdims=True)
    m_ref[...] = m_new
    return alpha, p


def _decode_init(q_ref, qrow_ref, m_ref, l_ref, acc_ref):
    n_rows = 2 * N_HEADS
    m_ref[...] = jnp.full(m_ref.shape, -jnp.inf, F32)
    l_ref[...] = jnp.zeros(l_ref.shape, F32)
    acc_ref[...] = jnp.zeros(acc_ref.shape, F32)
    row = lax.broadcasted_iota(jnp.int32, (n_rows, D_MODEL), 0)
    lane = lax.broadcasted_iota(jnp.int32, (n_rows, D_MODEL), 1)
    block = (row & (N_HEADS - 1)) * 2 + (row >> 3)
    qrow_ref[...] = jnp.where((lane >> 6) == block, q_ref[0], 0.0)


def _decode_pages(k_refs, v_refs, e_ref, qrow_ref, m_ref, l_ref, acc_ref):
    n_rows = 2 * N_HEADS
    npg = len(k_refs)
    qb = qrow_ref[...].astype(BF16)
    kcat = jnp.concatenate([r[0, 0].astype(BF16) for r in k_refs], axis=1)
    alpha, p = _softmax_step(_dot(qb, kcat), m_ref, l_ref)
    pb = p.astype(BF16)
    pstack = jnp.concatenate([pb[:, i * PAGE_SIZE:(i + 1) * PAGE_SIZE] for i in range(npg)], axis=0)
    pexp = _dot(pstack, e_ref[...])
    row = lax.broadcasted_iota(jnp.int32, pexp.shape, 0)
    lane = lax.broadcasted_iota(jnp.int32, pexp.shape, 1)
    own_head = (lane & (N_HEADS - 1)) == (row & (N_HEADS - 1))
    pexp = jnp.where(own_head, pexp, 0.0).astype(BF16)
    pcat = jnp.concatenate([pexp[i * n_rows:(i + 1) * n_rows] for i in range(npg)], axis=1)
    vcat = jnp.concatenate([r[0, 0].astype(BF16) for r in v_refs], axis=0)
    acc_ref[...] = alpha * acc_ref[...] + _dot(pcat, vcat)


def _decode_finish(wl_ref, g_ref, kn_ref, vn_ref, o_ref, qrow_ref, m_ref, l_ref, acc_ref, lam_init):
    s = jnp.sum(qrow_ref[...] * kn_ref[0], axis=-1, keepdims=True)
    m_prev = m_ref[...]
    m_new = jnp.maximum(m_prev, s)
    alpha = jnp.exp(m_prev - m_new)
    p = jnp.exp(s - m_new[:, :1])
    l = alpha * l_ref[...] + p
    vn = vn_ref[0]
    acc = alpha * acc_ref[...] + p * jnp.concatenate([vn, vn], axis=0)
    out = acc / l
    lam = _diff_lambda(wl_ref[...], lam_init)
    o = out[:N_HEADS] - lam * out[N_HEADS:]
    o_ref[0] = _sub_norm(o, g_ref[...], lam_init).astype(o_ref.dtype)


def _attn_kernel(pt_ref, wl_ref, g_ref, q_ref, k_ref, vt_ref, qs_ref, kn_ref, vn_ref, e_ref, *rest,
                 tq, kb, lam_init, steps_per_seq):
    npg = DEC_SUB * DEC_NP
    kp_refs, vp_refs = rest[:npg], rest[npg:2 * npg]
    o_ref, os_ref, s_ref, p_ref, acc_ref, qrow_ref, dm_ref, dl_ref, dacc_ref = rest[2 * npg:]
    qi = pl.program_id(2)
    step = (pl.program_id(0) * pl.num_programs(1) + pl.program_id(1)) * pl.num_programs(2) + qi
    phase = step % steps_per_seq
    dec_state = (qrow_ref, dm_ref, dl_ref, dacc_ref)

    @pl.when(phase == 0)
    def _():
        _decode_init(qs_ref, *dec_state)

    for u in range(DEC_SUB):
        _decode_pages(kp_refs[u * DEC_NP:(u + 1) * DEC_NP], vp_refs[u * DEC_NP:(u + 1) * DEC_NP], e_ref, *dec_state)

    q = q_ref[...].reshape(2 * tq, V_DIM)
    acc_ref[...] = jnp.zeros(acc_ref.shape, F32)
    nkb = tq // kb

    def scores(j):
        k = k_ref[0, pl.ds(pl.multiple_of(j * tq, tq), tq), :]
        st = _dot_nt(k, q)
        s_ref[...] = st
        return jnp.max(st, axis=0, keepdims=True)

    def values(j, alpha):
        vt = jnp.concatenate([vt_ref[0, j * nkb + b] for b in range(nkb)], axis=1)
        vt = jnp.concatenate([vt, jnp.ones((ONES_ROWS, tq), BF16)], axis=0)
        acc_ref[...] = alpha * acc_ref[...] + _dot(vt, p_ref[...])

    def body(j, carry):
        m, mt = carry
        m_new = jnp.maximum(m, mt)
        alpha = jnp.exp2(m - m_new)
        p_ref[...] = jnp.exp2(s_ref[...] - m_new).astype(BF16)
        mt_next = scores(j + 1)
        values(j, alpha)
        return m_new, mt_next

    init = (jnp.full((1, 2 * tq), -jnp.inf, F32), scores(0))
    m, _ = lax.fori_loop(0, qi, body, init)

    st = s_ref[...]
    key = lax.broadcasted_iota(jnp.int32, st.shape, 0)
    qry = lax.broadcasted_iota(jnp.int32, st.shape, 1)
    qry = jnp.where(qry >= tq, qry - tq, qry)
    st = jnp.where(key <= qry, st, -jnp.inf)
    m_new = jnp.maximum(m, jnp.max(st, axis=0, keepdims=True))
    p_ref[...] = jnp.exp2(st - m_new).astype(BF16)
    values(qi, jnp.exp2(m - m_new))

    acc = acc_ref[...]
    out_t = acc[:V_DIM] / acc[V_DIM:V_DIM + 1]
    lam = _diff_lambda(wl_ref[...], lam_init)
    o = (out_t[:, :tq] - lam * out_t[:, tq:]).T
    o_ref[...] = _sub_norm(o, g_ref[...], lam_init).astype(o_ref.dtype)

    @pl.when(phase == steps_per_seq - 1)
    def _():
        _decode_finish(wl_ref, g_ref, kn_ref, vn_ref, os_ref, *dec_state, lam_init)


def _attn_call(qm, khm, vt, w_lambda, subln, page_table, q_s, k_s, v_s, cache_k, cache_v, *, batch, seq, lam_init):
    tq = ATT_TQ
    nq = seq // tq
    kb = vt.shape[-1]
    nseq, n_pages = page_table.shape
    npg = DEC_SUB * DEC_NP
    steps_per_seq = n_pages // npg
    assert batch * N_HEADS * nq == nseq * steps_per_seq
    step_of = lambda b, h, i: (b * N_HEADS + h) * nq + i
    seq_of = lambda b, h, i: step_of(b, h, i) // steps_per_seq
    row = pl.BlockSpec((1, 1, D_MODEL), lambda b, h, i, pt: (seq_of(b, h, i), 0, 0))
    heads = pl.BlockSpec((1, N_HEADS, V_DIM), lambda b, h, i, pt: (seq_of(b, h, i), 0, 0))

    def page_spec(j):
        def index(b, h, i, pt):
            s = step_of(b, h, i)
            return (0, pt[s // steps_per_seq, (s % steps_per_seq) * npg + j], 0, 0)
        return pl.BlockSpec((1, 1, D_MODEL, PAGE_SIZE), index)

    pages = [page_spec(j) for j in range(npg)]
    expand = (jnp.arange(PAGE_SIZE * N_HEADS)[None, :] // N_HEADS == jnp.arange(PAGE_SIZE)[:, None]).astype(BF16)
    dec_rows = 2 * N_HEADS
    grid_spec = pltpu.PrefetchScalarGridSpec(
        num_scalar_prefetch=1,
        grid=(batch, N_HEADS, nq),
        in_specs=[pl.BlockSpec((4, HEAD_DIM), lambda b, h, i, pt: (0, 0)),
                  pl.BlockSpec((1, V_DIM), lambda b, h, i, pt: (0, 0)),
                  pl.BlockSpec((2, 1, tq, V_DIM), lambda b, h, i, pt: (0, h, b * nq + i, 0)),
                  pl.BlockSpec((1, seq, V_DIM), lambda b, h, i, pt: (h, b, 0)),
                  pl.BlockSpec((1, seq // kb, V_DIM, kb), lambda b, h, i, pt: (h, b, 0, 0)),
                  row, row, heads,
                  pl.BlockSpec((PAGE_SIZE, PAGE_SIZE * N_HEADS), lambda b, h, i, pt: (0, 0))] + pages + pages,
        out_specs=[pl.BlockSpec((tq, V_DIM), lambda b, h, i, pt: (b * nq + i, h)), heads],
        scratch_shapes=[pltpu.VMEM((tq, 2 * tq), F32), pltpu.VMEM((tq, 2 * tq), BF16),
                        pltpu.VMEM((V_DIM + ONES_ROWS, 2 * tq), F32),
                        pltpu.VMEM((dec_rows, D_MODEL), F32), pltpu.VMEM((dec_rows, LANES), F32),
                        pltpu.VMEM((dec_rows, LANES), F32), pltpu.VMEM((dec_rows, V_DIM), F32)],
    )
    r3 = lambda a: a.reshape(nseq, 1, D_MODEL)
    o_p, o_s = pl.pallas_call(
        functools.partial(_attn_kernel, tq=tq, kb=kb, lam_init=lam_init, steps_per_seq=steps_per_seq),
        grid_spec=grid_spec,
        out_shape=[jax.ShapeDtypeStruct((batch * seq, D_MODEL), BF16),
                   jax.ShapeDtypeStruct((nseq, N_HEADS, V_DIM), BF16)],
        compiler_params=_cparams("arbitrary", "arbitrary", "arbitrary"),
        name="attn",
    )(page_table, w_lambda, subln, qm, khm, vt, r3(q_s), r3(k_s), v_s.reshape(nseq, N_HEADS, V_DIM), expand,
      *([cache_k] * npg), *([cache_v] * npg))
    return o_p, o_s.reshape(nseq, D_MODEL)


def _proj_ln_kernel(o_ref, w_ref, x_ref, g_ref, b_ref, out_ref):
    z = ALPHA * x_ref[...] + _dot(o_ref[...], w_ref[...])
    out_ref[...] = _layer_norm(z, g_ref[...], b_ref[...])


def _proj_ln_call(o, w_bf16, x, g, b, *, tm):
    t = x.shape[0]
    row = pl.BlockSpec((tm, D_MODEL), lambda i: (i, 0))
    vec = pl.BlockSpec((1, D_MODEL), lambda i: (0, 0))
    return pl.pallas_call(
        _proj_ln_kernel,
        grid=(t // tm,),
        in_specs=[row, pl.BlockSpec((D_MODEL, D_MODEL), lambda i: (0, 0)), row, vec, vec],
        out_specs=row,
        out_shape=jax.ShapeDtypeStruct((t, D_MODEL), F32),
        compiler_params=_cparams("parallel"),
        name="proj_ln",
    )(o, w_bf16, x, g, b)


def _route(scores, sel):
    def pair_max(vals):
        best = None
        for i in range(len(vals)):
            for j in range(i + 1, len(vals)):
                s = vals[i] + vals[j]
                best = s if best is None else jnp.maximum(best, s)
        return best

    grp = [sel[g * EXPERTS_PER_GROUP:(g + 1) * EXPERTS_PER_GROUP] for g in range(N_EXPERT_GROUPS)]
    grp_score = [pair_max(v) for v in grp]
    best, g_idx = grp_score[0], jnp.zeros_like(grp_score[0], dtype=jnp.int32)
    for g in range(1, N_EXPERT_GROUPS):
        upd = grp_score[g] > best
        g_idx = jnp.where(upd, g, g_idx)
        best = jnp.where(upd, grp_score[g], best)

    def pick(rows, j):
        out = rows[j]
        for g in range(1, N_EXPERT_GROUPS):
            out = jnp.where(g_idx == g, rows[g * EXPERTS_PER_GROUP + j], out)
        return out

    in_sel = [pick(sel, j) for j in range(EXPERTS_PER_GROUP)]
    in_sc = [pick(scores, j) for j in range(EXPERTS_PER_GROUP)]

    def arg_first_max(vals, excluded):
        bv, bi = None, None
        for j, v in enumerate(vals):
            v = v if excluded is None else jnp.where(excluded == j, -jnp.inf, v)
            if bv is None:
                bv, bi = v, jnp.zeros_like(v, dtype=jnp.int32)
            else:
                upd = v > bv
                bi = jnp.where(upd, j, bi)
                bv = jnp.where(upd, v, bv)
        return bi

    l1 = arg_first_max(in_sel, None)
    l2 = arg_first_max(in_sel, l1)

    def take(vals, idx):
        out = vals[0]
        for j in range(1, len(vals)):
            out = jnp.where(idx == j, vals[j], out)
        return out

    w1, w2 = take(in_sc, l1), take(in_sc, l2)
    tot = w1 + w2
    w1, w2 = w1 / tot, w2 / tot
    gates = []
    for e in range(N_EXPERTS):
        g, j = divmod(e, EXPERTS_PER_GROUP)
        local = jnp.where(l1 == j, w1, jnp.where(l2 == j, w2, 0.0))
        gates.append(jnp.where(g_idx == g, local, 0.0))
    return gates


def _moe_ln_kernel(x_ref, wr_ref, rb_ref, win_ref, wout_ref, g_ref, b_ref, out_ref, hcat_ref):
    x = x_ref[...]
    tm = x.shape[0]
    xb = x.astype(BF16)
    x_lo = (x - xb.astype(F32)).astype(BF16)
    wr = wr_ref[...]
    hi_hilo = _dot(xb, wr)
    logits = hi_hilo[:, :LANES] + hi_hilo[:, LANES:] + _dot(x_lo, wr[:, :LANES])
    scores_t = jax.nn.sigmoid(logits.T[:N_EXPERTS])
    sel_t = scores_t + rb_ref[...]
    rows = lambda a: [a[e:e + 1] for e in range(N_EXPERTS)]
    gates = _route(rows(scores_t), rows(sel_t))
    gates_t = jnp.concatenate(gates + [jnp.zeros((LANES - N_EXPERTS, tm), F32)], axis=0)
    gate_cols = gates_t.T
    for e in range(N_EXPERTS):
        h = _dot(xb, win_ref[e])
        a = jax.nn.silu(h[:, :D_EXPERT]) * h[:, D_EXPERT:] * gate_cols[:, e:e + 1]
        hcat_ref[:, e * D_EXPERT:(e + 1) * D_EXPERT] = a.astype(BF16)
    z = ALPHA * x + _dot(hcat_ref[...], wout_ref[...])
    out_ref[...] = _layer_norm(z, g_ref[...], b_ref[...])


def _moe_ln_call(x, wr_pad, rb_col, win_bf16, wout_bf16, g, b, *, tm, layer):
    t = x.shape[0]
    row = pl.BlockSpec((tm, D_MODEL), lambda i: (i, 0))
    vec = pl.BlockSpec((1, D_MODEL), lambda i: (0, 0))
    once = pl.Buffered(1)
    return pl.pallas_call(
        _moe_ln_kernel,
        grid=(t // tm,),
        in_specs=[row,
                  pl.BlockSpec((D_MODEL, 2 * LANES), lambda i: (0, 0)),
                  pl.BlockSpec((N_EXPERTS, 1), lambda i: (0, 0)),
                  pl.BlockSpec((None, N_EXPERTS, D_MODEL, 2 * D_EXPERT), lambda i: (layer, 0, 0, 0),
                               pipeline_mode=once),
                  pl.BlockSpec((None, N_EXPERTS * D_EXPERT, D_MODEL), lambda i: (layer, 0, 0), pipeline_mode=once),
                  vec, vec],
        out_specs=row,
        out_shape=jax.ShapeDtypeStruct((t, D_MODEL), F32),
        scratch_shapes=[pltpu.VMEM((tm, N_EXPERTS * D_EXPERT), BF16)],
        compiler_params=_cparams("parallel"),
        name="moe_ln",
    )(x, wr_pad, rb_col, win_bf16, wout_bf16, g, b)


def _ssm_param_kernel(ar_ref, ai_ref, ls_ref, bre_ref, bim_ref,
                      abr_ref, abi_ref, anr_ref, ani_ref, wre_ref, wim_ref, *, n_sq):
    ar, ai = ar_ref[...], ai_ref[...]
    dt = jnp.exp(ls_ref[...])
    mag = jnp.exp(ar * dt)
    abr, abi = mag * jnp.cos(ai * dt), mag * jnp.sin(ai * dt)
    den = ar * ar + ai * ai
    cr = ((abr - 1.0) * ar + abi * ai) / den
    ci = (abi * ar - (abr - 1.0) * ai) / den
    bre, bim = bre_ref[...], bim_ref[...]
    wre_ref[...] = cr * bre - ci * bim
    wim_ref[...] = cr * bim + ci * bre
    abr_ref[...] = abr
    abi_ref[...] = abi
    pr, pi = abr, abi
    for _ in range(n_sq):
        pr, pi = pr * pr - pi * pi, 2.0 * pr * pi
    anr_ref[...] = pr
    ani_ref[...] = pi


def _ssm_param_call(a_re, a_im, log_step, b_re, b_im):
    col = lambda a: a.reshape(N_STATES, 1)
    ls = jnp.broadcast_to(log_step[:, None], (N_SSM_GROUPS, SSM_STATE))
    wide = lambda a: a.reshape(N_STATES, SSM_GROUP)
    cshape = jax.ShapeDtypeStruct((N_STATES, 1), F32)
    wshape = jax.ShapeDtypeStruct((N_STATES, SSM_GROUP), F32)
    return pl.pallas_call(
        functools.partial(_ssm_param_kernel, n_sq=int(math.log2(SSM_NA))),
        out_shape=[cshape] * 4 + [wshape] * 2,
        compiler_params=pltpu.CompilerParams(vmem_limit_bytes=VMEM_LIMIT),
        name="ssm_params",
    )(col(a_re), col(a_im), col(ls), wide(b_re), wide(b_im))


def _block_diag_in(w):
    gl = N_SSM_GROUPS // SSM_BLK
    w = w.reshape(SSM_BLK, gl, SSM_STATE, SSM_GROUP)
    eye = jnp.eye(gl, dtype=w.dtype)
    full = w.transpose(0, 1, 3, 2)[:, :, :, None, :] * eye[None, :, None, :, None]
    return full.reshape(SSM_BLK, gl * SSM_GROUP, gl * SSM_STATE)


def _block_diag_out(c):
    gl = N_SSM_GROUPS // SSM_BLK
    c = c.reshape(SSM_BLK, gl, SSM_GROUP, SSM_STATE)
    eye = jnp.eye(gl, dtype=c.dtype)
    full = c.transpose(0, 1, 3, 2)[:, :, :, None, :] * eye[None, :, None, :, None]
    return full.reshape(SSM_BLK, gl * SSM_STATE, gl * SSM_GROUP)


def _glu_ln(u, y, d_ref, wglu_ref, g_ref, b_ref):
    y = y + d_ref[...] * u
    gl = jax.nn.gelu(y).astype(BF16)
    vg = _dot(gl, wglu_ref[...])
    mix = vg[:, :D_MODEL] * jax.nn.sigmoid(vg[:, D_MODEL:])
    return _layer_norm(ALPHA * u + mix, g_ref[...], b_ref[...])


def _ssm_scan_kernel(u_ref, perm_ref, unperm_ref, abr_ref, abi_ref, anr_ref, ani_ref, wb_ref, wc_ref, d_ref,
                     wglu_ref, g_ref, b_ref, out_ref, hre_ref, him_ref, sre_scr, sim_scr, cre_scr, cim_scr):
    chunk = pl.program_id(1)
    na = SSM_NA

    @pl.when(chunk == 0)
    def _():
        cre_scr[...] = jnp.zeros(cre_scr.shape, F32)
        cim_scr[...] = jnp.zeros(cim_scr.shape, F32)

    u = u_ref[...]
    ub = _dot(perm_ref[...], u.astype(BF16)).astype(BF16)
    half = N_STATES // SSM_BLK
    per = half // SSM_LW
    bshape = (SUBLANES, SSM_LW)
    rows = lambda a: slice(a * SUBLANES, (a + 1) * SUBLANES)

    def recur(lg):
        ar = jnp.broadcast_to(abr_ref[lg], bshape)
        ai = jnp.broadcast_to(abi_ref[lg], bshape)
        er = ei = jnp.zeros(bshape, F32)
        for a in range(na):
            er, ei = (ar * er - ai * ei + sre_scr[lg, rows(a), :], ar * ei + ai * er + sim_scr[lg, rows(a), :])
        anr, ani = anr_ref[lg], ani_ref[lg]
        sr, si = cre_scr[lg], cim_scr[lg]
        starts_r, starts_i = [], []
        for r in range(SUBLANES):
            starts_r.append(sr)
            starts_i.append(si)
            sr, si = (anr * sr - ani * si + er[r:r + 1], anr * si + ani * sr + ei[r:r + 1])
        cre_scr[lg] = sr
        cim_scr[lg] = si
        hre_ref[0, :, lg * SSM_LW:(lg + 1) * SSM_LW] = sr
        him_ref[0, :, lg * SSM_LW:(lg + 1) * SSM_LW] = si
        hr, hi = jnp.concatenate(starts_r, axis=0), jnp.concatenate(starts_i, axis=0)
        for a in range(na):
            hr, hi = (ar * hr - ai * hi + sre_scr[lg, rows(a), :], ar * hi + ai * hr + sim_scr[lg, rows(a), :])
            sre_scr[lg, rows(a), :] = hr
            sim_scr[lg, rows(a), :] = hi

    ys = []
    for j in range(SSM_BLK):
        bu = _dot(ub[:, j * MXU_DIM:(j + 1) * MXU_DIM], wb_ref[j])
        for w in range(per):
            lg = j * per + w
            sre_scr[lg] = bu[:, w * SSM_LW:(w + 1) * SSM_LW]
            sim_scr[lg] = bu[:, half + w * SSM_LW:half + (w + 1) * SSM_LW]
            recur(lg)
        parts = [sre_scr[j * per + w] for w in range(per)] + [sim_scr[j * per + w] for w in range(per)]
        hcat = jnp.concatenate(parts, axis=1).astype(BF16)
        ys.append(_dot(hcat, wc_ref[j]))
    y = jnp.concatenate(ys, axis=1)
    y_hi = y.astype(BF16)
    y_lo = (y - y_hi.astype(F32)).astype(BF16)
    y = _dot(unperm_ref[...], y_hi) + _dot(unperm_ref[...], y_lo)
    out_ref[...] = _glu_ln(u, y, d_ref, wglu_ref, g_ref, b_ref)


def _ssm_scan_call(u, abr, abi, anr, ani, wb, wc, d_skip, wglu, g, b, *, batch, seq):
    nchunk = seq // SSM_TC
    row = pl.BlockSpec((SSM_TC, D_MODEL), lambda bb, c: (bb * nchunk + c, 0))
    vec = pl.BlockSpec((1, D_MODEL), lambda bb, c: (0, 0))
    lanes3 = pl.BlockSpec((SSM_NLG, 1, SSM_LW), lambda bb, c: (0, 0, 0))
    state = pl.BlockSpec((1, 1, N_STATES), lambda bb, c: (bb, 0, 0))
    square = pl.BlockSpec((SSM_TC, SSM_TC), lambda bb, c: (0, 0))
    sshape = jax.ShapeDtypeStruct((batch, 1, N_STATES), F32)
    g3 = lambda a: a.reshape(SSM_NLG, 1, SSM_LW)
    rows = jnp.arange(SSM_TC)
    perm = (rows[None, :] == ((rows % SUBLANES) * SSM_NA + rows // SUBLANES)[:, None]).astype(BF16)
    out, sre, sim = pl.pallas_call(
        _ssm_scan_kernel,
        grid=(batch, nchunk),
        in_specs=[row, square, square, lanes3, lanes3, lanes3, lanes3,
                  pl.BlockSpec(wb.shape, lambda bb, c: (0, 0, 0)),
                  pl.BlockSpec(wc.shape, lambda bb, c: (0, 0, 0)),
                  vec, pl.BlockSpec((D_MODEL, 2 * D_MODEL), lambda bb, c: (0, 0)), vec, vec],
        out_specs=[row, state, state],
        out_shape=[jax.ShapeDtypeStruct((batch * seq, D_MODEL), F32), sshape, sshape],
        scratch_shapes=[pltpu.VMEM((SSM_NLG, SSM_TC, SSM_LW), F32),
                        pltpu.VMEM((SSM_NLG, SSM_TC, SSM_LW), F32),
                        pltpu.VMEM((SSM_NLG, 1, SSM_LW), F32),
                        pltpu.VMEM((SSM_NLG, 1, SSM_LW), F32)],
        compiler_params=_cparams("parallel", "arbitrary"),
        name="ssm_scan",
    )(u, perm, perm.T, g3(abr), g3(abi), g3(anr), g3(ani), wb, wc, d_skip, wglu, g, b)
    return out, sre, sim


def _ssm_step_kernel(u_ref, h0r_ref, h0i_ref, abr_ref, abi_ref, wb_ref, wc_ref, d_ref, wglu_ref, g_ref, b_ref,
                     out_ref, hre_ref, him_ref):
    u = u_ref[...]
    half = N_STATES // SSM_BLK
    abr, abi = abr_ref[...], abi_ref[...]
    h0r, h0i = h0r_ref[...], h0i_ref[...]
    ys = []
    for j in range(SSM_BLK):
        bu = jnp.dot(u[:, j * MXU_DIM:(j + 1) * MXU_DIM], wb_ref[j], preferred_element_type=F32,
                     precision=lax.Precision.HIGHEST)
        cols = slice(j * half, (j + 1) * half)
        hr = abr[:, cols] * h0r[:, cols] - abi[:, cols] * h0i[:, cols] + bu[:, :half]
        hi = abr[:, cols] * h0i[:, cols] + abi[:, cols] * h0r[:, cols] + bu[:, half:]
        hre_ref[:, cols] = hr
        him_ref[:, cols] = hi
        hcat = jnp.concatenate([hr, hi], axis=1).astype(BF16)
        ys.append(_dot(hcat, wc_ref[j]))
    y = jnp.concatenate(ys, axis=1)
    out_ref[...] = _glu_ln(u, y, d_ref, wglu_ref, g_ref, b_ref)


def _ssm_step_call(u, h0r, h0i, abr, abi, wb_f32, wc, d_skip, wglu, g, b):
    n = u.shape[0]
    sshape = jax.ShapeDtypeStruct((n, N_STATES), F32)
    return pl.pallas_call(
        _ssm_step_kernel,
        out_shape=[jax.ShapeDtypeStruct((n, D_MODEL), F32), sshape, sshape],
        compiler_params=pltpu.CompilerParams(vmem_limit_bytes=VMEM_LIMIT),
        name="ssm_step",
    )(u, h0r, h0i, abr, abi, wb_f32, wc, d_skip, wglu, g, b)


def kernel(x_prompt, x_sample, cache_k, cache_v, state_ssm_re, state_ssm_im, page_table, w_qkv, w_lambda,
           attn_subln, w_attn_out, ssm_a_re, ssm_a_im, ssm_log_step, ssm_b_re, ssm_b_im, ssm_c_re, ssm_c_im,
           ssm_d, w_glu, w_router, router_bias, w_expert_in, w_expert_out, ln_gain, ln_bias):
    bp, lp, _ = x_prompt.shape
    bs, ls, _ = x_sample.shape
    assert ls == 1 and lp % SSM_TC == 0 and lp % ATT_TQ == 0
    n_pages = page_table.shape[1]
    past_len = n_pages * PAGE_SIZE
    n_pool = cache_k.shape[1]
    tp = bp * lp
    xp = x_prompt.reshape(tp, D_MODEL)
    xs = x_sample.reshape(bs, D_MODEL)

    vec = lambda a: a.reshape(1, D_MODEL)
    wr_hi = w_router.astype(BF16)
    wr_lo = (w_router - wr_hi.astype(F32)).astype(BF16)
    lane_pad = lambda a: jnp.pad(a, ((0, 0), (0, LANES - N_EXPERTS)))
    wr_pad = jnp.concatenate([lane_pad(wr_hi), lane_pad(wr_lo)], axis=1)
    rb_col = router_bias.reshape(N_EXPERTS, 1)

    win = w_expert_in.astype(BF16)
    wout = w_expert_out.astype(BF16).reshape(DEPTH, N_EXPERTS * D_EXPERT, D_MODEL)

    def moe(x, layer, tm):
        n = x.shape[0]
        pad = -n % tm
        xpad = jnp.pad(x, ((0, pad), (0, 0))) if pad else x
        out = _moe_ln_call(xpad, wr_pad, rb_col, win, wout, vec(ln_gain[layer, 1]), vec(ln_bias[layer, 1]),
                           tm=tm, layer=layer)
        return out[:n] if pad else out

    lam_init = 0.8 - 0.6 * math.exp(-0.3 * 0)
    wqkv = w_qkv[0].astype(BF16)
    wo = w_attn_out[0].astype(BF16)
    subln = attn_subln[0].reshape(1, V_DIM)
    tab_p = _rope_tables(jnp.arange(lp, dtype=jnp.int32))
    tab_s = _rope_tables(jnp.broadcast_to(past_len + jnp.arange(ls, dtype=jnp.int32), (bs,)))
    kt_p, v_p, qm, khm, vt = _qkv_call(xp, wqkv, tab_p, tm=QKV_TM, seq=lp, prompt=True)
    k_s, v_s, q_s = _qkv_call(xs, wqkv, tab_s, tm=bs, seq=bs, prompt=False)
    ck = cache_k[0].transpose(0, 2, 3, 4, 1).reshape(1, n_pool, D_MODEL, PAGE_SIZE)
    cv = cache_v[0].reshape(1, n_pool, PAGE_SIZE * N_HEADS, V_DIM)
    o_p, o_s = _attn_call(qm, khm, vt, w_lambda[0], subln, page_table, q_s, k_s, v_s, ck, cv,
                          batch=bp, seq=lp, lam_init=lam_init)
    g0, b0 = vec(ln_gain[0, 0]), vec(ln_bias[0, 0])
    hp = _proj_ln_call(o_p, wo, xp, g0, b0, tm=ROW_TM)
    hs = _proj_ln_call(o_s, wo, xs, g0, b0, tm=bs)
    hp = moe(hp, 0, MOE_TM)
    hs = moe(hs, 0, LANES)

    abr, abi, anr, ani, wre, wim = _ssm_param_call(ssm_a_re[0], ssm_a_im[0], ssm_log_step[0],
                                                   ssm_b_re[0], ssm_b_im[0])
    lane_row = lambda a: a.reshape(1, N_STATES)
    wb = jnp.concatenate([_block_diag_in(wre), _block_diag_in(wim)], axis=-1)
    wc = jnp.concatenate([_block_diag_out(ssm_c_re[0]), -_block_diag_out(ssm_c_im[0])], axis=1)
    wc = wc.astype(BF16)
    wglu = w_glu[0].astype(BF16)
    d_skip = vec(ssm_d[0])
    g1, b1 = vec(ln_gain[1, 0]), vec(ln_bias[1, 0])
    hp, sre_p, sim_p = _ssm_scan_call(hp, abr, abi, anr, ani, wb.astype(BF16), wc, d_skip, wglu, g1, b1,
                                      batch=bp, seq=lp)
    hs, sre_s, sim_s = _ssm_step_call(hs, state_ssm_re[0].reshape(bs, N_STATES),
                                      state_ssm_im[0].reshape(bs, N_STATES),
                                      lane_row(abr), lane_row(abi), wb, wc, d_skip, wglu, g1, b1)
    hp = moe(hp, 1, MOE_TM)
    hs = moe(hs, 1, LANES)

    st = lambda a, n: a.reshape(1, n, N_SSM_GROUPS, SSM_STATE)
    return (hp.reshape(bp, lp, D_MODEL), hs.reshape(bs, ls, D_MODEL),
            kt_p.reshape(1, bp, N_HEADS, 2, HEAD_DIM, lp).transpose(0, 1, 5, 2, 3, 4),
            v_p.reshape(1, bp, lp, N_HEADS, V_DIM),
            st(sre_p, bp), st(sim_p, bp),
            k_s.reshape(1, bs, ls, N_HEADS, 2, HEAD_DIM), v_s.reshape(1, bs, ls, N_HEADS, V_DIM),
            st(sre_s, bs), st(sim_s, bs))
```

```python
import functools
import math

import jax
import jax.numpy as jnp
from jax import lax
from jax.experimental import pallas as pl
from jax.experimental.pallas import tpu as pltpu

F32 = jnp.float32
BF16 = jnp.bfloat16

D_MODEL = 1024
DEPTH = 2
PAGE_SIZE = 128
N_HEADS = 8
HEAD_DIM = 64
V_DIM = 2 * HEAD_DIM
ROT_DIM = HEAD_DIM // 4
ROPE_THETA = 500000.0
SSM_GROUP = 16
N_SSM_GROUPS = D_MODEL // SSM_GROUP
SSM_STATE = 64
N_STATES = N_SSM_GROUPS * SSM_STATE
N_EXPERTS = 16
N_EXPERT_GROUPS = 4
EXPERTS_PER_GROUP = N_EXPERTS // N_EXPERT_GROUPS
D_EXPERT = 256
ALPHA = (2 * DEPTH) ** 0.25
LN_EPS = 1e-5
QK_SCALE = HEAD_DIM ** -0.5
LOG2E = math.log2(math.e)

LANES = 128
SUBLANES = 8
MXU_DIM = 256
VMEM_LIMIT = 56 * 1024 * 1024

QKV_TM = 256
ATT_TQ = 512
MOE_TM = 256
SSM_TC = 256
SSM_NA = SSM_TC // SUBLANES
SSM_LW = 512
SSM_NLG = N_STATES // SSM_LW
SSM_BLK = 4
DEC_NP = 16
DEC_SUB = 1
ONES_ROWS = 16


def _cparams(*sem):
    return pltpu.CompilerParams(dimension_semantics=sem, vmem_limit_bytes=VMEM_LIMIT)


def _layer_norm(z, g, b):
    mu = jnp.mean(z, axis=-1, keepdims=True)
    d = z - mu
    var = jnp.mean(d * d, axis=-1, keepdims=True)
    return d * lax.rsqrt(var + LN_EPS) * g + b


def _dot(a, b):
    return jnp.dot(a, b, preferred_element_type=F32)


def _dot_nt(a, b):
    return lax.dot_general(a, b, (((1,), (1,)), ((), ())), preferred_element_type=F32)


def _rope(x, cos, s_up, s_dn):
    outs = []
    for h in range(N_HEADS):
        xh = x[:, h * V_DIM:(h + 1) * V_DIM]
        up = pltpu.roll(xh, V_DIM - ROT_DIM // 2, 1)
        dn = pltpu.roll(xh, ROT_DIM // 2, 1)
        outs.append(xh * cos + up * s_up + dn * s_dn)
    return outs


def _qkv_kernel(x_ref, w_ref, cos_ref, sup_ref, sdn_ref, *out_refs, prompt):
    y = _dot(x_ref[...].astype(BF16), w_ref[...])
    cos, s_up, s_dn = cos_ref[...], sup_ref[...], sdn_ref[...]
    q = _rope(y[:, :D_MODEL], cos, s_up, s_dn)
    k = _rope(y[:, D_MODEL:2 * D_MODEL], cos, s_up, s_dn)
    v = y[:, 2 * D_MODEL:]
    if prompt:
        kt_ref, v_ref, qm_ref, khm_ref, vt_ref = out_refs
        lane = lax.broadcasted_iota(jnp.int32, (1, V_DIM), 1)
        first = lane < HEAD_DIM
        for h in range(N_HEADS):
            cols = slice(h * V_DIM, (h + 1) * V_DIM)
            qh = q[h] * (QK_SCALE * LOG2E)
            qm_ref[0, h] = jnp.where(first, qh, 0.0).astype(BF16)
            qm_ref[1, h] = jnp.where(first, 0.0, qh).astype(BF16)
            khm_ref[h] = k[h].astype(BF16)
            kt_ref[0, cols, :] = k[h].T
            vt_ref[h, 0] = v[:, cols].T.astype(BF16)
    else:
        k_ref, v_ref, q_ref = out_refs
        for h in range(N_HEADS):
            cols = slice(h * V_DIM, (h + 1) * V_DIM)
            q_ref[:, cols] = q[h] * QK_SCALE
            k_ref[:, cols] = k[h]
    v_ref[...] = v


def _qkv_call(x, w_bf16, tables, *, tm, seq, prompt):
    t = x.shape[0]
    nblk = seq // tm
    row = pl.BlockSpec((tm, D_MODEL), lambda i: (i, 0))
    tab = pl.BlockSpec((tm, V_DIM), lambda i: (i % nblk, 0))
    rows_f32 = jax.ShapeDtypeStruct((t, D_MODEL), F32)
    if prompt:
        hm = pl.BlockSpec((N_HEADS, tm, V_DIM), lambda i: (0, i, 0))
        out_shape = [jax.ShapeDtypeStruct((t // seq, D_MODEL, seq), F32), rows_f32,
                     jax.ShapeDtypeStruct((2, N_HEADS, t, V_DIM), BF16),
                     jax.ShapeDtypeStruct((N_HEADS, t, V_DIM), BF16),
                     jax.ShapeDtypeStruct((N_HEADS, t // tm, V_DIM, tm), BF16)]
        out_specs = [pl.BlockSpec((1, D_MODEL, tm), lambda i: (i // nblk, 0, i % nblk)), row,
                     pl.BlockSpec((2, N_HEADS, tm, V_DIM), lambda i: (0, 0, i, 0)), hm,
                     pl.BlockSpec((N_HEADS, 1, V_DIM, tm), lambda i: (0, i, 0, 0))]
    else:
        out_shape = [rows_f32] * 3
        out_specs = [row] * 3
    return pl.pallas_call(
        functools.partial(_qkv_kernel, prompt=prompt),
        grid=(t // tm,),
        in_specs=[row, pl.BlockSpec((D_MODEL, 3 * D_MODEL), lambda i: (0, 0)), tab, tab, tab],
        out_specs=out_specs,
        out_shape=out_shape,
        compiler_params=_cparams("parallel"),
        name="qkv_rope",
    )(x, w_bf16, *tables)


def _rope_tables(pos):
    half = ROT_DIM // 2
    inv = ROPE_THETA ** (-jnp.arange(half, dtype=F32) * 2.0 / ROT_DIM)
    ang = pos.astype(F32)[:, None] * inv
    cos, sin = jnp.cos(ang), jnp.sin(ang)
    n = pos.shape[0]
    ones = jnp.ones((n, HEAD_DIM - ROT_DIM), F32)
    zeros = jnp.zeros((n, HEAD_DIM - ROT_DIM), F32)
    z8 = jnp.zeros((n, half), F32)
    c = jnp.concatenate([cos, cos, ones], axis=1)
    s_up = jnp.concatenate([-sin, z8, zeros], axis=1)
    s_dn = jnp.concatenate([z8, sin, zeros], axis=1)
    return tuple(jnp.tile(a, (1, 2)) for a in (c, s_up, s_dn))


def _diff_lambda(wl, lam_init):
    a = jnp.sum(wl[0:1] * wl[1:2], axis=-1, keepdims=True)
    b = jnp.sum(wl[2:3] * wl[3:4], axis=-1, keepdims=True)
    return jnp.exp(a) - jnp.exp(b) + lam_init


def _sub_norm(o, g, lam_init):
    o = o * lax.rsqrt(jnp.mean(o * o, axis=-1, keepdims=True) + LN_EPS)
    return o * g * (1.0 - lam_init)


def _softmax_step(s, m_ref, l_ref):
    m_prev = m_ref[...]
    m_new = jnp.maximum(m_prev, jnp.max(s, axis=-1, keepdims=True))
    alpha = jnp.exp(m_prev - m_new)
    p = jnp.exp(s - m_new[:, :1])
    l_ref[...] = alpha * l_ref[...] + jnp.sum(p, axis=-1, keepdims=True)
    m_ref[...] = m_new
    return alpha, p


def _decode_init(q_ref, qrow_ref, m_ref, l_ref, acc_ref):
    n_rows = 2 * N_HEADS
    m_ref[...] = jnp.full(m_ref.shape, -jnp.inf, F32)
    l_ref[...] = jnp.zeros(l_ref.shape, F32)
    acc_ref[...] = jnp.zeros(acc_ref.shape, F32)
    row = lax.broadcasted_iota(jnp.int32, (n_rows, D_MODEL), 0)
    lane = lax.broadcasted_iota(jnp.int32, (n_rows, D_MODEL), 1)
    block = (row & (N_HEADS - 1)) * 2 + (row >> 3)
    qrow_ref[...] = jnp.where((lane >> 6) == block, q_ref[0], 0.0)


def _decode_pages(k_refs, v_refs, e_ref, qrow_ref, m_ref, l_ref, acc_ref):
    n_rows = 2 * N_HEADS
    npg = len(k_refs)
    qb = qrow_ref[...].astype(BF16)
    kcat = jnp.concatenate([r[0, 0].astype(BF16) for r in k_refs], axis=1)
    alpha, p = _softmax_step(_dot(qb, kcat), m_ref, l_ref)
    pb = p.astype(BF16)
    pstack = jnp.concatenate([pb[:, i * PAGE_SIZE:(i + 1) * PAGE_SIZE] for i in range(npg)], axis=0)
    pexp = _dot(pstack, e_ref[...])
    row = lax.broadcasted_iota(jnp.int32, pexp.shape, 0)
    lane = lax.broadcasted_iota(jnp.int32, pexp.shape, 1)
    own_head = (lane & (N_HEADS - 1)) == (row & (N_HEADS - 1))
    pexp = jnp.where(own_head, pexp, 0.0).astype(BF16)
    pcat = jnp.concatenate([pexp[i * n_rows:(i + 1) * n_rows] for i in range(npg)], axis=1)
    vcat = jnp.concatenate([r[0, 0].astype(BF16) for r in v_refs], axis=0)
    acc_ref[...] = alpha * acc_ref[...] + _dot(pcat, vcat)


def _decode_finish(wl_ref, g_ref, kn_ref, vn_ref, o_ref, qrow_ref, m_ref, l_ref, acc_ref, lam_init):
    s = jnp.sum(qrow_ref[...] * kn_ref[0], axis=-1, keepdims=True)
    m_prev = m_ref[...]
    m_new = jnp.maximum(m_prev, s)
    alpha = jnp.exp(m_prev - m_new)
    p = jnp.exp(s - m_new[:, :1])
    l = alpha * l_ref[...] + p
    vn = vn_ref[0]
    acc = alpha * acc_ref[...] + p * jnp.concatenate([vn, vn], axis=0)
    out = acc / l
    lam = _diff_lambda(wl_ref[...], lam_init)
    o = out[:N_HEADS] - lam * out[N_HEADS:]
    o_ref[0] = _sub_norm(o, g_ref[...], lam_init).astype(o_ref.dtype)


def _attn_kernel(pt_ref, wl_ref, g_ref, q_ref, k_ref, vt_ref, qs_ref, kn_ref, vn_ref, e_ref, *rest,
                 tq, kb, lam_init, steps_per_seq):
    npg = DEC_SUB * DEC_NP
    kp_refs, vp_refs = rest[:npg], rest[npg:2 * npg]
    o_ref, os_ref, s_ref, p_ref, acc_ref, qrow_ref, dm_ref, dl_ref, dacc_ref = rest[2 * npg:]
    qi = pl.program_id(2)
    step = (pl.program_id(0) * pl.num_programs(1) + pl.program_id(1)) * pl.num_programs(2) + qi
    phase = step % steps_per_seq
    dec_state = (qrow_ref, dm_ref, dl_ref, dacc_ref)

    @pl.when(phase == 0)
    def _():
        _decode_init(qs_ref, *dec_state)

    for u in range(DEC_SUB):
        _decode_pages(kp_refs[u * DEC_NP:(u + 1) * DEC_NP], vp_refs[u * DEC_NP:(u + 1) * DEC_NP], e_ref, *dec_state)

    q = q_ref[...].reshape(2 * tq, V_DIM)
    acc_ref[...] = jnp.zeros(acc_ref.shape, F32)
    nkb = tq // kb

    def scores(j):
        k = k_ref[0, pl.ds(pl.multiple_of(j * tq, tq), tq), :]
        st = _dot_nt(k, q)
        s_ref[...] = st
        return jnp.max(st, axis=0, keepdims=True)

    def values(j, alpha):
        vt = jnp.concatenate([vt_ref[0, j * nkb + b] for b in range(nkb)], axis=1)
        vt = jnp.concatenate([vt, jnp.ones((ONES_ROWS, tq), BF16)], axis=0)
        acc_ref[...] = alpha * acc_ref[...] + _dot(vt, p_ref[...])

    def body(j, carry):
        m, mt = carry
        m_new = jnp.maximum(m, mt)
        alpha = jnp.exp2(m - m_new)
        p_ref[...] = jnp.exp2(s_ref[...] - m_new).astype(BF16)
        mt_next = scores(j + 1)
        values(j, alpha)
        return m_new, mt_next

    init = (jnp.full((1, 2 * tq), -jnp.inf, F32), scores(0))
    m, _ = lax.fori_loop(0, qi, body, init)

    st = s_ref[...]
    key = lax.broadcasted_iota(jnp.int32, st.shape, 0)
    qry = lax.broadcasted_iota(jnp.int32, st.shape, 1)
    qry = jnp.where(qry >= tq, qry - tq, qry)
    st = jnp.where(key <= qry, st, -jnp.inf)
    m_new = jnp.maximum(m, jnp.max(st, axis=0, keepdims=True))
    p_ref[...] = jnp.exp2(st - m_new).astype(BF16)
    values(qi, jnp.exp2(m - m_new))

    acc = acc_ref[...]
    out_t = acc[:V_DIM] / acc[V_DIM:V_DIM + 1]
    lam = _diff_lambda(wl_ref[...], lam_init)
    o = (out_t[:, :tq] - lam * out_t[:, tq:]).T
    o_ref[...] = _sub_norm(o, g_ref[...], lam_init).astype(o_ref.dtype)

    @pl.when(phase == steps_per_seq - 1)
    def _():
        _decode_finish(wl_ref, g_ref, kn_ref, vn_ref, os_ref, *dec_state, lam_init)


def _attn_call(qm, khm, vt, w_lambda, subln, page_table, q_s, k_s, v_s, cache_k, cache_v, *, batch, seq, lam_init):
    tq = ATT_TQ
    nq = seq // tq
    kb = vt.shape[-1]
    nseq, n_pages = page_table.shape
    npg = DEC_SUB * DEC_NP
    steps_per_seq = n_pages // npg
    assert batch * N_HEADS * nq == nseq * steps_per_seq
    step_of = lambda b, h, i: (b * N_HEADS + h) * nq + i
    seq_of = lambda b, h, i: step_of(b, h, i) // steps_per_seq
    row = pl.BlockSpec((1, 1, D_MODEL), lambda b, h, i, pt: (seq_of(b, h, i), 0, 0))
    heads = pl.BlockSpec((1, N_HEADS, V_DIM), lambda b, h, i, pt: (seq_of(b, h, i), 0, 0))

    def page_spec(j):
        def index(b, h, i, pt):
            s = step_of(b, h, i)
            return (0, pt[s // steps_per_seq, (s % steps_per_seq) * npg + j], 0, 0)
        return pl.BlockSpec((1, 1, D_MODEL, PAGE_SIZE), index)

    pages = [page_spec(j) for j in range(npg)]
    expand = (jnp.arange(PAGE_SIZE * N_HEADS)[None, :] // N_HEADS == jnp.arange(PAGE_SIZE)[:, None]).astype(BF16)
    dec_rows = 2 * N_HEADS
    grid_spec = pltpu.PrefetchScalarGridSpec(
        num_scalar_prefetch=1,
        grid=(batch, N_HEADS, nq),
        in_specs=[pl.BlockSpec((4, HEAD_DIM), lambda b, h, i, pt: (0, 0)),
                  pl.BlockSpec((1, V_DIM), lambda b, h, i, pt: (0, 0)),
                  pl.BlockSpec((2, 1, tq, V_DIM), lambda b, h, i, pt: (0, h, b * nq + i, 0)),
                  pl.BlockSpec((1, seq, V_DIM), lambda b, h, i, pt: (h, b, 0)),
                  pl.BlockSpec((1, seq // kb, V_DIM, kb), lambda b, h, i, pt: (h, b, 0, 0)),
                  row, row, heads,
                  pl.BlockSpec((PAGE_SIZE, PAGE_SIZE * N_HEADS), lambda b, h, i, pt: (0, 0))] + pages + pages,
        out_specs=[pl.BlockSpec((tq, V_DIM), lambda b, h, i, pt: (b * nq + i, h)), heads],
        scratch_shapes=[pltpu.VMEM((tq, 2 * tq), F32), pltpu.VMEM((tq, 2 * tq), BF16),
                        pltpu.VMEM((V_DIM + ONES_ROWS, 2 * tq), F32),
                        pltpu.VMEM((dec_rows, D_MODEL), F32), pltpu.VMEM((dec_rows, LANES), F32),
                        pltpu.VMEM((dec_rows, LANES), F32), pltpu.VMEM((dec_rows, V_DIM), F32)],
    )
    r3 = lambda a: a.reshape(nseq, 1, D_MODEL)
    o_p, o_s = pl.pallas_call(
        functools.partial(_attn_kernel, tq=tq, kb=kb, lam_init=lam_init, steps_per_seq=steps_per_seq),
        grid_spec=grid_spec,
        out_shape=[jax.ShapeDtypeStruct((batch * seq, D_MODEL), BF16),
                   jax.ShapeDtypeStruct((nseq, N_HEADS, V_DIM), BF16)],
        compiler_params=_cparams("arbitrary", "arbitrary", "arbitrary"),
        name="attn",
    )(page_table, w_lambda, subln, qm, khm, vt, r3(q_s), r3(k_s), v_s.reshape(nseq, N_HEADS, V_DIM), expand,
      *([cache_k] * npg), *([cache_v] * npg))
    return o_p, o_s.reshape(nseq, D_MODEL)


def _route(scores, sel):
    def pair_max(vals):
        best = None
        for i in range(len(vals)):
            for j in range(i + 1, len(vals)):
                s = vals[i] + vals[j]
                best = s if best is None else jnp.maximum(best, s)
        return best

    grp = [sel[g * EXPERTS_PER_GROUP:(g + 1) * EXPERTS_PER_GROUP] for g in range(N_EXPERT_GROUPS)]
    grp_score = [pair_max(v) for v in grp]
    best, g_idx = grp_score[0], jnp.zeros_like(grp_score[0], dtype=jnp.int32)
    for g in range(1, N_EXPERT_GROUPS):
        upd = grp_score[g] > best
        g_idx = jnp.where(upd, g, g_idx)
        best = jnp.where(upd, grp_score[g], best)

    def pick(rows, j):
        out = rows[j]
        for g in range(1, N_EXPERT_GROUPS):
            out = jnp.where(g_idx == g, rows[g * EXPERTS_PER_GROUP + j], out)
        return out

    in_sel = [pick(sel, j) for j in range(EXPERTS_PER_GROUP)]
    in_sc = [pick(scores, j) for j in range(EXPERTS_PER_GROUP)]

    def arg_first_max(vals, excluded):
        bv, bi = None, None
        for j, v in enumerate(vals):
            v = v if excluded is None else jnp.where(excluded == j, -jnp.inf, v)
            if bv is None:
                bv, bi = v, jnp.zeros_like(v, dtype=jnp.int32)
            else:
                upd = v > bv
                bi = jnp.where(upd, j, bi)
                bv = jnp.where(upd, v, bv)
        return bi

    l1 = arg_first_max(in_sel, None)
    l2 = arg_first_max(in_sel, l1)

    def take(vals, idx):
        out = vals[0]
        for j in range(1, len(vals)):
            out = jnp.where(idx == j, vals[j], out)
        return out

    w1, w2 = take(in_sc, l1), take(in_sc, l2)
    tot = w1 + w2
    w1, w2 = w1 / tot, w2 / tot
    gates = []
    for e in range(N_EXPERTS):
        g, j = divmod(e, EXPERTS_PER_GROUP)
        local = jnp.where(l1 == j, w1, jnp.where(l2 == j, w2, 0.0))
        gates.append(jnp.where(g_idx == g, local, 0.0))
    return gates


def _moe_ln_kernel(*refs, after_attention):
    if after_attention:
        o_ref, wo_ref, xin_ref, g0_ref, b0_ref = refs[:5]
        refs = refs[5:]
        x = _layer_norm(ALPHA * xin_ref[...] + _dot(o_ref[...], wo_ref[...]), g0_ref[...], b0_ref[...])
    else:
        x = refs[0][...]
        refs = refs[1:]
    wr_ref, rb_ref, win_ref, wout_ref, g_ref, b_ref, out_ref, hcat_ref = refs
    tm = x.shape[0]
    xb = x.astype(BF16)
    x_lo = (x - xb.astype(F32)).astype(BF16)
    wr = wr_ref[...]
    hi_hilo = _dot(xb, wr)
    logits = hi_hilo[:, :LANES] + hi_hilo[:, LANES:] + _dot(x_lo, wr[:, :LANES])
    scores_t = jax.nn.sigmoid(logits.T[:N_EXPERTS])
    sel_t = scores_t + rb_ref[...]
    rows = lambda a: [a[e:e + 1] for e in range(N_EXPERTS)]
    gates = _route(rows(scores_t), rows(sel_t))
    gates_t = jnp.concatenate(gates + [jnp.zeros((LANES - N_EXPERTS, tm), F32)], axis=0)
    gate_cols = gates_t.T
    for e in range(N_EXPERTS):
        h = _dot(xb, win_ref[e])
        a = jax.nn.silu(h[:, :D_EXPERT]) * h[:, D_EXPERT:] * gate_cols[:, e:e + 1]
        hcat_ref[:, e * D_EXPERT:(e + 1) * D_EXPERT] = a.astype(BF16)
    z = ALPHA * x + _dot(hcat_ref[...], wout_ref[...])
    out_ref[...] = _layer_norm(z, g_ref[...], b_ref[...])


def _moe_ln_call(x_in, wr_pad, rb_col, win_bf16, wout_bf16, g, b, *, tm, layer):
    after_attention = isinstance(x_in, tuple)
    row = pl.BlockSpec((tm, D_MODEL), lambda i: (i, 0))
    vec = pl.BlockSpec((1, D_MODEL), lambda i: (0, 0))
    once = pl.Buffered(1)
    if after_attention:
        lead = list(x_in)
        lead_specs = [row, pl.BlockSpec((D_MODEL, D_MODEL), lambda i: (0, 0), pipeline_mode=once), row, vec, vec]
        t = x_in[2].shape[0]
    else:
        lead, lead_specs, t = [x_in], [row], x_in.shape[0]
    return pl.pallas_call(
        functools.partial(_moe_ln_kernel, after_attention=after_attention),
        grid=(t // tm,),
        in_specs=lead_specs + [
                  pl.BlockSpec((D_MODEL, 2 * LANES), lambda i: (0, 0)),
                  pl.BlockSpec((N_EXPERTS, 1), lambda i: (0, 0)),
                  pl.BlockSpec((None, N_EXPERTS, D_MODEL, 2 * D_EXPERT), lambda i: (layer, 0, 0, 0),
                               pipeline_mode=once),
                  pl.BlockSpec((None, N_EXPERTS * D_EXPERT, D_MODEL), lambda i: (layer, 0, 0), pipeline_mode=once),
                  vec, vec],
        out_specs=row,
        out_shape=jax.ShapeDtypeStruct((t, D_MODEL), F32),
        scratch_shapes=[pltpu.VMEM((tm, N_EXPERTS * D_EXPERT), BF16)],
        compiler_params=_cparams("parallel"),
        name="moe_ln",
    )(*lead, wr_pad, rb_col, win_bf16, wout_bf16, g, b)


def _ssm_param_kernel(ar_ref, ai_ref, ls_ref, bre_ref, bim_ref,
                      abr_ref, abi_ref, anr_ref, ani_ref, wre_ref, wim_ref, *, n_sq):
    ar, ai = ar_ref[...], ai_ref[...]
    dt = jnp.exp(ls_ref[...])
    mag = jnp.exp(ar * dt)
    abr, abi = mag * jnp.cos(ai * dt), mag * jnp.sin(ai * dt)
    den = ar * ar + ai * ai
    cr = ((abr - 1.0) * ar + abi * ai) / den
    ci = (abi * ar - (abr - 1.0) * ai) / den
    bre, bim = bre_ref[...], bim_ref[...]
    wre_ref[...] = cr * bre - ci * bim
    wim_ref[...] = cr * bim + ci * bre
    abr_ref[...] = abr
    abi_ref[...] = abi
    pr, pi = abr, abi
    for _ in range(n_sq):
        pr, pi = pr * pr - pi * pi, 2.0 * pr * pi
    anr_ref[...] = pr
    ani_ref[...] = pi


def _ssm_param_call(a_re, a_im, log_step, b_re, b_im):
    col = lambda a: a.reshape(N_STATES, 1)
    ls = jnp.broadcast_to(log_step[:, None], (N_SSM_GROUPS, SSM_STATE))
    wide = lambda a: a.reshape(N_STATES, SSM_GROUP)
    cshape = jax.ShapeDtypeStruct((N_STATES, 1), F32)
    wshape = jax.ShapeDtypeStruct((N_STATES, SSM_GROUP), F32)
    return pl.pallas_call(
        functools.partial(_ssm_param_kernel, n_sq=int(math.log2(SSM_NA))),
        out_shape=[cshape] * 4 + [wshape] * 2,
        compiler_params=pltpu.CompilerParams(vmem_limit_bytes=VMEM_LIMIT),
        name="ssm_params",
    )(col(a_re), col(a_im), col(ls), wide(b_re), wide(b_im))


def _block_diag_in(w):
    gl = N_SSM_GROUPS // SSM_BLK
    w = w.reshape(SSM_BLK, gl, SSM_STATE, SSM_GROUP)
    eye = jnp.eye(gl, dtype=w.dtype)
    full = w.transpose(0, 1, 3, 2)[:, :, :, None, :] * eye[None, :, None, :, None]
    return full.reshape(SSM_BLK, gl * SSM_GROUP, gl * SSM_STATE)


def _block_diag_out(c):
    gl = N_SSM_GROUPS // SSM_BLK
    c = c.reshape(SSM_BLK, gl, SSM_GROUP, SSM_STATE)
    eye = jnp.eye(gl, dtype=c.dtype)
    full = c.transpose(0, 1, 3, 2)[:, :, :, None, :] * eye[None, :, None, :, None]
    return full.reshape(SSM_BLK, gl * SSM_STATE, gl * SSM_GROUP)


def _glu_ln(u, y, d_ref, wglu_ref, g_ref, b_ref):
    y = y + d_ref[...] * u
    gl = jax.nn.gelu(y).astype(BF16)
    vg = _dot(gl, wglu_ref[...])
    mix = vg[:, :D_MODEL] * jax.nn.sigmoid(vg[:, D_MODEL:])
    return _layer_norm(ALPHA * u + mix, g_ref[...], b_ref[...])


def _ssm_scan_kernel(u_ref, perm_ref, unperm_ref, abr_ref, abi_ref, anr_ref, ani_ref, wb_ref, wc_ref, d_ref,
                     wglu_ref, g_ref, b_ref, out_ref, hre_ref, him_ref, sre_scr, sim_scr, cre_scr, cim_scr):
    chunk = pl.program_id(1)
    na = SSM_NA

    @pl.when(chunk == 0)
    def _():
        cre_scr[...] = jnp.zeros(cre_scr.shape, F32)
        cim_scr[...] = jnp.zeros(cim_scr.shape, F32)

    u = u_ref[...]
    ub = _dot(perm_ref[...], u.astype(BF16)).astype(BF16)
    half = N_STATES // SSM_BLK
    per = half // SSM_LW
    bshape = (SUBLANES, SSM_LW)
    rows = lambda a: slice(a * SUBLANES, (a + 1) * SUBLANES)

    def recur(lg):
        ar = jnp.broadcast_to(abr_ref[lg], bshape)
        ai = jnp.broadcast_to(abi_ref[lg], bshape)
        er = ei = jnp.zeros(bshape, F32)
        for a in range(na):
            er, ei = (ar * er - ai * ei + sre_scr[lg, rows(a), :], ar * ei + ai * er + sim_scr[lg, rows(a), :])
        anr, ani = anr_ref[lg], ani_ref[lg]
        sr, si = cre_scr[lg], cim_scr[lg]
        starts_r, starts_i = [], []
        for r in range(SUBLANES):
            starts_r.append(sr)
            starts_i.append(si)
            sr, si = (anr * sr - ani * si + er[r:r + 1], anr * si + ani * sr + ei[r:r + 1])
        cre_scr[lg] = sr
        cim_scr[lg] = si
        hre_ref[0, :, lg * SSM_LW:(lg + 1) * SSM_LW] = sr
        him_ref[0, :, lg * SSM_LW:(lg + 1) * SSM_LW] = si
        hr, hi = jnp.concatenate(starts_r, axis=0), jnp.concatenate(starts_i, axis=0)
        for a in range(na):
            hr, hi = (ar * hr - ai * hi + sre_scr[lg, rows(a), :], ar * hi + ai * hr + sim_scr[lg, rows(a), :])
            sre_scr[lg, rows(a), :] = hr
            sim_scr[lg, rows(a), :] = hi

    ys = []
    for j in range(SSM_BLK):
        bu = _dot(ub[:, j * MXU_DIM:(j + 1) * MXU_DIM], wb_ref[j])
        for w in range(per):
            lg = j * per + w
            sre_scr[lg] = bu[:, w * SSM_LW:(w + 1) * SSM_LW]
            sim_scr[lg] = bu[:, half + w * SSM_LW:half + (w + 1) * SSM_LW]
            recur(lg)
        parts = [sre_scr[j * per + w] for w in range(per)] + [sim_scr[j * per + w] for w in range(per)]
        hcat = jnp.concatenate(parts, axis=1).astype(BF16)
        ys.append(_dot(hcat, wc_ref[j]))
    y = jnp.concatenate(ys, axis=1)
    y_hi = y.astype(BF16)
    y_lo = (y - y_hi.astype(F32)).astype(BF16)
    y = _dot(unperm_ref[...], y_hi) + _dot(unperm_ref[...], y_lo)
    out_ref[...] = _glu_ln(u, y, d_ref, wglu_ref, g_ref, b_ref)


def _ssm_scan_call(u, abr, abi, anr, ani, wb, wc, d_skip, wglu, g, b, *, batch, seq):
    nchunk = seq // SSM_TC
    row = pl.BlockSpec((SSM_TC, D_MODEL), lambda bb, c: (bb * nchunk + c, 0))
    vec = pl.BlockSpec((1, D_MODEL), lambda bb, c: (0, 0))
    lanes3 = pl.BlockSpec((SSM_NLG, 1, SSM_LW), lambda bb, c: (0, 0, 0))
    state = pl.BlockSpec((1, 1, N_STATES), lambda bb, c: (bb, 0, 0))
    square = pl.BlockSpec((SSM_TC, SSM_TC), lambda bb, c: (0, 0))
    sshape = jax.ShapeDtypeStruct((batch, 1, N_STATES), F32)
    g3 = lambda a: a.reshape(SSM_NLG, 1, SSM_LW)
    rows = jnp.arange(SSM_TC)
    perm = (rows[None, :] == ((rows % SUBLANES) * SSM_NA + rows // SUBLANES)[:, None]).astype(BF16)
    out, sre, sim = pl.pallas_call(
        _ssm_scan_kernel,
        grid=(batch, nchunk),
        in_specs=[row, square, square, lanes3, lanes3, lanes3, lanes3,
                  pl.BlockSpec(wb.shape, lambda bb, c: (0, 0, 0)),
                  pl.BlockSpec(wc.shape, lambda bb, c: (0, 0, 0)),
                  vec, pl.BlockSpec((D_MODEL, 2 * D_MODEL), lambda bb, c: (0, 0)), vec, vec],
        out_specs=[row, state, state],
        out_shape=[jax.ShapeDtypeStruct((batch * seq, D_MODEL), F32), sshape, sshape],
        scratch_shapes=[pltpu.VMEM((SSM_NLG, SSM_TC, SSM_LW), F32),
                        pltpu.VMEM((SSM_NLG, SSM_TC, SSM_LW), F32),
                        pltpu.VMEM((SSM_NLG, 1, SSM_LW), F32),
                        pltpu.VMEM((SSM_NLG, 1, SSM_LW), F32)],
        compiler_params=_cparams("parallel", "arbitrary"),
        name="ssm_scan",
    )(u, perm, perm.T, g3(abr), g3(abi), g3(anr), g3(ani), wb, wc, d_skip, wglu, g, b)
    return out, sre, sim


def _ssm_step_kernel(u_ref, h0r_ref, h0i_ref, abr_ref, abi_ref, wb_ref, wc_ref, d_ref, wglu_ref, g_ref, b_ref,
                     out_ref, hre_ref, him_ref):
    u = u_ref[...]
    half = N_STATES // SSM_BLK
    abr, abi = abr_ref[...], abi_ref[...]
    h0r, h0i = h0r_ref[...], h0i_ref[...]
    ys = []
    for j in range(SSM_BLK):
        bu = jnp.dot(u[:, j * MXU_DIM:(j + 1) * MXU_DIM], wb_ref[j], preferred_element_type=F32,
                     precision=lax.Precision.HIGHEST)
        cols = slice(j * half, (j + 1) * half)
        hr = abr[:, cols] * h0r[:, cols] - abi[:, cols] * h0i[:, cols] + bu[:, :half]
        hi = abr[:, cols] * h0i[:, cols] + abi[:, cols] * h0r[:, cols] + bu[:, half:]
        hre_ref[:, cols] = hr
        him_ref[:, cols] = hi
        hcat = jnp.concatenate([hr, hi], axis=1).astype(BF16)
        ys.append(_dot(hcat, wc_ref[j]))
    y = jnp.concatenate(ys, axis=1)
    out_ref[...] = _glu_ln(u, y, d_ref, wglu_ref, g_ref, b_ref)


def _ssm_step_call(u, h0r, h0i, abr, abi, wb_f32, wc, d_skip, wglu, g, b):
    n = u.shape[0]
    sshape = jax.ShapeDtypeStruct((n, N_STATES), F32)
    return pl.pallas_call(
        _ssm_step_kernel,
        out_shape=[jax.ShapeDtypeStruct((n, D_MODEL), F32), sshape, sshape],
        compiler_params=pltpu.CompilerParams(vmem_limit_bytes=VMEM_LIMIT),
        name="ssm_step",
    )(u, h0r, h0i, abr, abi, wb_f32, wc, d_skip, wglu, g, b)


def kernel(x_prompt, x_sample, cache_k, cache_v, state_ssm_re, state_ssm_im, page_table, w_qkv, w_lambda,
           attn_subln, w_attn_out, ssm_a_re, ssm_a_im, ssm_log_step, ssm_b_re, ssm_b_im, ssm_c_re, ssm_c_im,
           ssm_d, w_glu, w_router, router_bias, w_expert_in, w_expert_out, ln_gain, ln_bias):
    bp, lp, _ = x_prompt.shape
    bs, ls, _ = x_sample.shape
    assert ls == 1 and lp % SSM_TC == 0 and lp % ATT_TQ == 0
    n_pages = page_table.shape[1]
    past_len = n_pages * PAGE_SIZE
    n_pool = cache_k.shape[1]
    tp = bp * lp
    xp = x_prompt.reshape(tp, D_MODEL)
    xs = x_sample.reshape(bs, D_MODEL)

    vec = lambda a: a.reshape(1, D_MODEL)
    wr_hi = w_router.astype(BF16)
    wr_lo = (w_router - wr_hi.astype(F32)).astype(BF16)
    lane_pad = lambda a: jnp.pad(a, ((0, 0), (0, LANES - N_EXPERTS)))
    wr_pad = jnp.concatenate([lane_pad(wr_hi), lane_pad(wr_lo)], axis=1)
    rb_col = router_bias.reshape(N_EXPERTS, 1)

    win = w_expert_in.astype(BF16)
    wout = w_expert_out.astype(BF16).reshape(DEPTH, N_EXPERTS * D_EXPERT, D_MODEL)

    def moe(x, layer, tm, attn=None):
        n = x.shape[0]
        pad = -n % tm
        rows = lambda a: jnp.pad(a, ((0, pad), (0, 0))) if pad else a
        x_in = rows(x) if attn is None else (rows(attn[0]), attn[1], rows(x), attn[2], attn[3])
        out = _moe_ln_call(x_in, wr_pad, rb_col, win, wout, vec(ln_gain[layer, 1]), vec(ln_bias[layer, 1]),
                           tm=tm, layer=layer)
        return out[:n] if pad else out

    lam_init = 0.8 - 0.6 * math.exp(-0.3 * 0)
    wqkv = w_qkv[0].astype(BF16)
    wo = w_attn_out[0].astype(BF16)
    subln = attn_subln[0].reshape(1, V_DIM)
    tab_p = _rope_tables(jnp.arange(lp, dtype=jnp.int32))
    tab_s = _rope_tables(jnp.broadcast_to(past_len + jnp.arange(ls, dtype=jnp.int32), (bs,)))
    kt_p, v_p, qm, khm, vt = _qkv_call(xp, wqkv, tab_p, tm=QKV_TM, seq=lp, prompt=True)
    k_s, v_s, q_s = _qkv_call(xs, wqkv, tab_s, tm=bs, seq=bs, prompt=False)
    ck = cache_k[0].transpose(0, 2, 3, 4, 1).reshape(1, n_pool, D_MODEL, PAGE_SIZE)
    cv = cache_v[0].reshape(1, n_pool, PAGE_SIZE * N_HEADS, V_DIM)
    o_p, o_s = _attn_call(qm, khm, vt, w_lambda[0], subln, page_table, q_s, k_s, v_s, ck, cv,
                          batch=bp, seq=lp, lam_init=lam_init)
    g0, b0 = vec(ln_gain[0, 0]), vec(ln_bias[0, 0])
    hp = moe(xp, 0, MOE_TM, attn=(o_p, wo, g0, b0))
    hs = moe(xs, 0, LANES, attn=(o_s, wo, g0, b0))

    abr, abi, anr, ani, wre, wim = _ssm_param_call(ssm_a_re[0], ssm_a_im[0], ssm_log_step[0],
                                                   ssm_b_re[0], ssm_b_im[0])
    lane_row = lambda a: a.reshape(1, N_STATES)
    wb = jnp.concatenate([_block_diag_in(wre), _block_diag_in(wim)], axis=-1)
    wc = jnp.concatenate([_block_diag_out(ssm_c_re[0]), -_block_diag_out(ssm_c_im[0])], axis=1)
    wc = wc.astype(BF16)
    wglu = w_glu[0].astype(BF16)
    d_skip = vec(ssm_d[0])
    g1, b1 = vec(ln_gain[1, 0]), vec(ln_bias[1, 0])
    hp, sre_p, sim_p = _ssm_scan_call(hp, abr, abi, anr, ani, wb.astype(BF16), wc, d_skip, wglu, g1, b1,
                                      batch=bp, seq=lp)
    hs, sre_s, sim_s = _ssm_step_call(hs, state_ssm_re[0].reshape(bs, N_STATES),
                                      state_ssm_im[0].reshape(bs, N_STATES),
                                      lane_row(abr), lane_row(abi), wb, wc, d_skip, wglu, g1, b1)
    hp = moe(hp, 1, MOE_TM)
    hs = moe(hs, 1, LANES)

    st = lambda a, n: a.reshape(1, n, N_SSM_GROUPS, SSM_STATE)
    return (hp.reshape(bp, lp, D_MODEL), hs.reshape(bs, ls, D_MODEL),
            kt_p.reshape(1, bp, N_HEADS, 2, HEAD_DIM, lp).transpose(0, 1, 5, 2, 3, 4),
            v_p.reshape(1, bp, lp, N_HEADS, V_DIM),
            st(sre_p, bp), st(sim_p, bp),
            k_s.reshape(1, bs, ls, N_HEADS, 2, HEAD_DIM), v_s.reshape(1, bs, ls, N_HEADS, V_DIM),
            st(sre_s, bs), st(sim_s, bs))
```
